```python
import jax, jax.numpy as jnp
from jax import lax
import numpy as np

D_MODEL = 1024
BATCH = 2
SEQ = 16384
DEPTH = 2

CHUNK = 64
N_MIXERS = 2
N_RWKV_LAYERS = (DEPTH + 1) // 2
N_SB_LAYERS = DEPTH // 2

ALPHA = (2 * DEPTH) ** 0.25
BETA = (8 * DEPTH) ** -0.25
LN_EPS = 1e-5

RWKV_HEAD_DIM = 64
RWKV_HEADS = D_MODEL // RWKV_HEAD_DIM
DECAY_LORA = 64
ICLR_LORA = 64
GATE_LORA = 128
GN_EPS = 64e-5

SB_HEAD_DIM = 64
SB_HEADS = D_MODEL // SB_HEAD_DIM
SB_BLOCK = 128

N_EXPERTS = 32
TOP_K = 4
D_FF = D_MODEL
SWIGLU_LIMIT = 7.0
SWIGLU_ALPHA = 1.702
EXPERT_BLOCK = 128

kernel_name = 'rwkv7_stickbreak_moe_deepnorm_hybrid'


def layer_norm(x, g, b):
    xf = x.astype(jnp.float32)
    mu = jnp.mean(xf, axis=-1, keepdims=True)
    var = jnp.mean(jnp.square(xf - mu), axis=-1, keepdims=True)
    return ((xf - mu) * lax.rsqrt(var + LN_EPS) * g + b).astype(x.dtype)


def _rwkv7_scan(r, decay, k, v, a, b):
    B, T, H, N = r.shape

    def step(state, inp):
        r_t, d_t, k_t, v_t, a_t, b_t = inp
        sa = jnp.einsum('bhvk,bhk->bhv', state, a_t)
        state = (state * d_t[:, :, None, :] + sa[..., None] * b_t[:, :, None, :]
                 + v_t[..., None] * k_t[:, :, None, :])
        return state, jnp.einsum('bhvk,bhk->bhv', state, r_t)

    xs = tuple(jnp.moveaxis(t, 1, 0) for t in (r, decay, k, v, a, b))
    _, o = lax.scan(step, jnp.zeros((B, H, N, N), jnp.float32), xs)
    return jnp.moveaxis(o, 0, 1)


def rwkv7_time_mix(x, mix, w_rkv, w0, w1, w2, a0, a1, a2, g1, g2, k_k, k_a, r_k, lnx_g, lnx_b, w_o):
    B, T, D = x.shape
    H, N = RWKV_HEADS, RWKV_HEAD_DIM
    xx = jnp.pad(x, ((0, 0), (1, 0), (0, 0)))[:, :-1] - x
    xr, xw, xk, xv, xa, xg = (x + xx * mix[j] for j in range(6))
    r = xr @ w_rkv[0]
    k = xk @ w_rkv[1]
    v = xv @ w_rkv[2]
    w = -jax.nn.softplus(-(w0 + jnp.tanh(xw @ w1) @ w2)) - 0.5
    a = jax.nn.sigmoid(a0 + (xa @ a1) @ a2)
    g = jax.nn.sigmoid(xg @ g1) @ g2
    heads = lambda t: t.astype(jnp.float32).reshape(B, T, H, N)
    kk = heads(k * k_k)
    kk = kk / jnp.maximum(jnp.linalg.norm(kk, axis=-1, keepdims=True), 1e-12)
    k = k * (1.0 + (a - 1.0) * k_a)
    rh, kh, vh, ah = heads(r), heads(k), heads(v), heads(a)
    decay = jnp.exp(-jnp.exp(heads(w)))
    o = _rwkv7_scan(rh, decay, kh, vh, -kk, kk * ah)
    mu = jnp.mean(o, axis=-1, keepdims=True)
    var = jnp.mean(jnp.square(o - mu), axis=-1, keepdims=True)
    o = ((o - mu) * lax.rsqrt(var + GN_EPS)).reshape(B, T, D) * lnx_g + lnx_b
    bonus = jnp.sum(rh * kh * r_k, axis=-1, keepdims=True) * vh
    o = o + bonus.reshape(B, T, D)
    return (o.astype(x.dtype) * g) @ w_o


def stick_breaking_attention(x, w_qkv, w_o):
    B, T, D = x.shape
    H, N = SB_HEADS, SB_HEAD_DIM
    qkv = (x @ w_qkv).reshape(B, T, 3, H, N).astype(jnp.float32)
    q = jnp.transpose(qkv[:, :, 0], (0, 2, 1, 3)) * (N ** -0.5)
    k = jnp.transpose(qkv[:, :, 1], (0, 2, 1, 3))
    v = jnp.transpose(qkv[:, :, 2], (0, 2, 1, 3))
    outs = []
    for i in range(T // SB_BLOCK):
        lo, hi = i * SB_BLOCK, (i + 1) * SB_BLOCK
        z = jnp.einsum('bhqd,bhkd->bhqk', q[:, :, lo:hi], k[:, :, :hi])
        mask = jnp.arange(hi)[None, :] < jnp.arange(lo, hi)[:, None]
        log_keep = jnp.where(mask, jax.nn.log_sigmoid(-z), 0.0)
        between = lax.cumsum(log_keep, axis=3, reverse=True) - log_keep
        att = jnp.where(mask, jnp.exp(jax.nn.log_sigmoid(z) + between), 0.0)
        outs.append(jnp.einsum('bhqk,bhkd->bhqd', att, v[:, :, :hi]))
    o = jnp.concatenate(outs, axis=2)
    o = jnp.transpose(o, (0, 2, 1, 3)).reshape(B, T, D).astype(x.dtype)
    return o @ w_o


def clamped_swiglu(h):
    glu = jnp.minimum(h[..., ::2], SWIGLU_LIMIT)
    lin = jnp.clip(h[..., 1::2], -SWIGLU_LIMIT, SWIGLU_LIMIT)
    return glu * jax.nn.sigmoid(SWIGLU_ALPHA * glu) * (lin + 1.0)


def moe_ffn(x, w_router, b_router, w1, b1, w2, b2):
    B, T, D = x.shape
    xf = x.reshape(-1, D)
    n = xf.shape[0]
    logits = (xf @ w_router + b_router).astype(jnp.float32)
    top_logit, top_idx = lax.top_k(logits, TOP_K)
    top_w = jax.nn.softmax(top_logit, axis=-1)
    n_assign = n * TOP_K
    expert = top_idx.reshape(-1)
    token = jnp.arange(n_assign, dtype=jnp.int32) // TOP_K
    gate = top_w.reshape(-1)
    order = jnp.argsort(expert)
    expert_s = expert[order]
    counts = jnp.bincount(expert, length=N_EXPERTS)
    padded = (counts + EXPERT_BLOCK - 1) // EXPERT_BLOCK * EXPERT_BLOCK
    pad_end = jnp.cumsum(padded)
    pad_start = pad_end - padded
    start = jnp.cumsum(counts) - counts
    dest = pad_start[expert_s] + jnp.arange(n_assign) - start[expert_s]
    n_blocks = -(-n_assign // EXPERT_BLOCK) + N_EXPERTS
    n_rows = n_blocks * EXPERT_BLOCK
    row_token = jnp.full((n_rows,), n, jnp.int32).at[dest].set(token[order])
    row_gate = jnp.zeros((n_rows,), jnp.float32).at[dest].set(gate[order])
    block_expert = jnp.minimum(
        jnp.searchsorted(pad_end, jnp.arange(n_blocks) * EXPERT_BLOCK, side='right'), N_EXPERTS - 1)
    x_pad = jnp.concatenate([xf, jnp.zeros((1, D), xf.dtype)], axis=0)

    def expert_block(args):
        rows, e = args
        h = x_pad[rows] @ w1[e] + b1[e]
        return clamped_swiglu(h) @ w2[e] + b2[e]

    y = lax.map(expert_block, (row_token.reshape(n_blocks, EXPERT_BLOCK), block_expert))
    y = y.reshape(n_rows, D) * row_gate[:, None].astype(y.dtype)
    out = jnp.zeros((n + 1, D), y.dtype).at[row_token].add(y)[:n]
    return out.reshape(B, T, D).astype(x.dtype)


def setup_inputs(seed: int = 0) -> dict:
    key = jax.random.key(seed)
    ks = iter(jax.random.split(key, 32))
    nrm = lambda shape, scale: scale * jax.random.normal(next(ks), shape, jnp.float32)
    uni = lambda shape, lo, hi: jax.random.uniform(next(ks), shape, jnp.float32, lo, hi)
    D, NR, NS, L, E, F = D_MODEL, N_RWKV_LAYERS, N_SB_LAYERS, DEPTH, N_EXPERTS, D_FF
    H, N = RWKV_HEADS, RWKV_HEAD_DIM
    rkv_scale = jnp.array([1.0, 1.0, BETA], jnp.float32)[None, :, None, None]
    qkv_scale = jnp.concatenate([jnp.ones((2 * D,), jnp.float32), jnp.full((D,), BETA, jnp.float32)])
    return {
        'x': nrm((BATCH, SEQ, D), 1.0),
        'rw_mix': uni((NR, 6, D), 0.0, 1.0),
        'rw_w_rkv': nrm((NR, 3, D, D), D ** -0.5) * rkv_scale,
        'rw_w0': uni((NR, D), -6.0, 1.0),
        'rw_w1': nrm((NR, D, DECAY_LORA), D ** -0.5),
        'rw_w2': nrm((NR, DECAY_LORA, D), 0.1 * DECAY_LORA ** -0.5),
        'rw_a0': nrm((NR, D), 0.1),
        'rw_a1': nrm((NR, D, ICLR_LORA), D ** -0.5),
        'rw_a2': nrm((NR, ICLR_LORA, D), 0.1 * ICLR_LORA ** -0.5),
        'rw_g1': nrm((NR, D, GATE_LORA), D ** -0.5),
        'rw_g2': nrm((NR, GATE_LORA, D), GATE_LORA ** -0.5),
        'rw_k_k': 0.85 + nrm((NR, D), 0.02),
        'rw_k_a': 1.0 + nrm((NR, D), 0.02),
        'rw_r_k': nrm((NR, H, N), 0.1),
        'rw_lnx_g': 1.0 + nrm((NR, D), 0.02),
        'rw_lnx_b': nrm((NR, D), 0.02),
        'rw_w_o': nrm((NR, D, D), BETA * D ** -0.5),
        'sb_w_qkv': nrm((NS, D, 3 * D), D ** -0.5) * qkv_scale,
        'sb_w_o': nrm((NS, D, D), BETA * D ** -0.5),
        'moe_w_router': nrm((L, D, E), D ** -0.5),
        'moe_b_router': nrm((L, E), 0.01),
        'moe_w1': nrm((L, E, D, 2 * F), D ** -0.5),
        'moe_b1': nrm((L, E, 2 * F), 0.01),
        'moe_w2': nrm((L, E, F, D), BETA * F ** -0.5),
        'moe_b2': nrm((L, E, D), 0.01),
        'ln_g': 1.0 + nrm((L, 2, D), 0.02),
        'ln_b': nrm((L, 2, D), 0.02),
    }


def reference(x, rw_mix, rw_w_rkv, rw_w0, rw_w1, rw_w2, rw_a0, rw_a1, rw_a2, rw_g1, rw_g2,
              rw_k_k, rw_k_a, rw_r_k, rw_lnx_g, rw_lnx_b, rw_w_o, sb_w_qkv, sb_w_o,
              moe_w_router, moe_b_router, moe_w1, moe_b1, moe_w2, moe_b2, ln_g, ln_b):
    for i in range(DEPTH):
        j = i // N_MIXERS
        if i % N_MIXERS == 0:
            h = rwkv7_time_mix(x, rw_mix[j], rw_w_rkv[j], rw_w0[j], rw_w1[j], rw_w2[j],
                               rw_a0[j], rw_a1[j], rw_a2[j], rw_g1[j], rw_g2[j],
                               rw_k_k[j], rw_k_a[j], rw_r_k[j], rw_lnx_g[j], rw_lnx_b[j], rw_w_o[j])
        else:
            h = stick_breaking_attention(x, sb_w_qkv[j], sb_w_o[j])
        x = layer_norm(ALPHA * x + h, ln_g[i, 0], ln_b[i, 0])
        f = moe_ffn(x, moe_w_router[i], moe_b_router[i], moe_w1[i], moe_b1[i], moe_w2[i], moe_b2[i])
        x = layer_norm(ALPHA * x + f, ln_g[i, 1], ln_b[i, 1])
    return x
```

```python
import functools

import jax
import jax.numpy as jnp
from jax import lax
from jax.experimental import pallas as pl
from jax.experimental.pallas import tpu as pltpu

F32 = jnp.float32
BF16 = jnp.bfloat16
HIGHEST = lax.Precision.HIGHEST

DEPTH = 2
ALPHA = (2 * DEPTH) ** 0.25
LN_EPS = 1e-5
GN_EPS = 64e-5
HEAD_DIM = 64
LANES = 128
TOP_K = 4
SWIGLU_LIMIT = 7.0
SWIGLU_ALPHA = 1.702

ROW_TILE = 256
SCAN_CHUNK = 64
SCAN_ROWS = 512
SB_BLOCK = 128
EXPERT_ROWS = 256
VMEM_LIMIT = 56 * 1024 * 1024


def _cparams(semantics):
    return pltpu.CompilerParams(dimension_semantics=semantics, vmem_limit_bytes=VMEM_LIMIT)


def _dot(a, b, precision=None):
    return jnp.dot(a, b, preferred_element_type=F32, precision=precision)


def _dot_nt(a, b, precision=None):
    return lax.dot_general(a, b, (((1,), (1,)), ((), ())), preferred_element_type=F32, precision=precision)


def _dot_tn(a, b, precision=None):
    return lax.dot_general(a, b, (((0,), (0,)), ((), ())), preferred_element_type=F32, precision=precision)


def _softplus(u):
    return jnp.maximum(u, 0.0) + jnp.log(1.0 + jnp.exp(-jnp.abs(u)))


def _sigmoid(u):
    return 1.0 / (1.0 + jnp.exp(-u))


def _layer_norm(y, g, b):
    mu = jnp.mean(y, axis=-1, keepdims=True)
    d = y - mu
    var = jnp.mean(d * d, axis=-1, keepdims=True)
    return d * lax.rsqrt(var + LN_EPS) * g + b


def _head_block_ones():
    r = lax.broadcasted_iota(jnp.int32, (LANES, LANES), 0) // HEAD_DIM
    c = lax.broadcasted_iota(jnp.int32, (LANES, LANES), 1) // HEAD_DIM
    return (r == c).astype(F32)


def _rwkv_proj_kernel(x_ref, xp_ref, mix_ref, wrkv_ref, w0_ref, w1_ref, w2_ref, a0_ref, a1_ref, a2_ref,
                      g1_ref, g2_ref, r_ref, k_ref, v_ref, ld_ref, a_ref, g_ref, *, tiles_per_seq):
    i = pl.program_id(0)
    x = x_ref[...]
    prev_last = jnp.where(i % tiles_per_seq == 0, 0.0, xp_ref[7:8, :])
    row = lax.broadcasted_iota(jnp.int32, x.shape, 0)
    x_prev = jnp.where(row == 0, prev_last, pltpu.roll(x, 1, 0))
    xx = x_prev - x

    def mixed(j):
        return (x + xx * mix_ref[j:j + 1, :]).astype(BF16)

    r_ref[...] = _dot(mixed(0), wrkv_ref[0])
    w = w0_ref[...] + _dot(jnp.tanh(_dot(mixed(1), w1_ref[...])).astype(BF16), w2_ref[...])
    ld_ref[...] = -jnp.exp(-_softplus(-w) - 0.5)
    k_ref[...] = _dot(mixed(2), wrkv_ref[1])
    v_ref[...] = _dot(mixed(3), wrkv_ref[2])
    a_ref[...] = _sigmoid(a0_ref[...] + _dot(_dot(mixed(4), a1_ref[...]).astype(BF16), a2_ref[...]))
    g_ref[...] = _dot(_sigmoid(_dot(mixed(5), g1_ref[...])).astype(BF16), g2_ref[...])


def _rwkv_proj(xf, seq_len, mix, w_rkv, w0, w1, w2, a0, a1, a2, g1, g2):
    n, d = xf.shape
    tm = min(ROW_TILE, seq_len)
    full = lambda shape: pl.BlockSpec(shape, lambda i: (0,) * len(shape))
    row_tile = pl.BlockSpec((tm, d), lambda i: (i, 0))
    prev_rows = pl.BlockSpec((8, d), lambda i: (jnp.maximum(i * (tm // 8) - 1, 0), 0))
    mix8 = jnp.concatenate([mix, jnp.zeros((2, d), F32)], axis=0)
    out = jax.ShapeDtypeStruct((n, d), F32)
    return pl.pallas_call(
        functools.partial(_rwkv_proj_kernel, tiles_per_seq=seq_len // tm),
        grid=(n // tm,),
        in_specs=[row_tile, prev_rows, full((8, d)), full(w_rkv.shape), full((1, d)), full(w1.shape),
                  full(w2.shape), full((1, d)), full(a1.shape), full(a2.shape), full(g1.shape), full(g2.shape)],
        out_specs=[row_tile] * 6,
        out_shape=[out] * 6,
        compiler_params=_cparams(("parallel",)),
        name="rwkv_proj",
    )(xf, xf, mix8, w_rkv.astype(BF16), w0.reshape(1, d), w1.astype(BF16), w2.astype(BF16),
      a0.reshape(1, d), a1.astype(BF16), a2.astype(BF16), g1.astype(BF16), g2.astype(BF16))


def _rwkv_scan_kernel(r_ref, k_ref, v_ref, ld_ref, a_ref, kk_ref, ka_ref, rk_ref, lg_ref, lb_ref,
                      o_ref, s_ref, *, chunk, n_chunks):
    L = chunk

    @pl.when(pl.program_id(1) == 0)
    def _():
        s_ref[...] = jnp.zeros_like(s_ref)

    head0 = lax.broadcasted_iota(jnp.int32, (1, LANES), 1) < HEAD_DIM
    row = lax.broadcasted_iota(jnp.int32, (2 * L, 2 * L), 0)
    col = lax.broadcasted_iota(jnp.int32, (2 * L, 2 * L), 1)
    same_head = (row // L) == (col // L)
    lower_strict = jnp.logical_and(same_head, col < row)
    lower_incl = jnp.logical_and(same_head, col <= row)
    eye = (row == col).astype(F32)
    cum_rows = (lax.broadcasted_iota(jnp.int32, (L, L), 1) <= lax.broadcasted_iota(jnp.int32, (L, L), 0)).astype(F32)
    head_ones = _head_block_ones()
    ones_ln = jnp.ones((L, LANES), F32)
    hp = HIGHEST

    def stack(t):
        return jnp.concatenate([jnp.where(head0, t, 0.0), jnp.where(head0, 0.0, t)], axis=0)

    def modified_key(k, a):
        return k * (1.0 + (a - 1.0) * ka_ref[...])

    def chunk_step(ci, carry):
        sl = pl.ds(pl.multiple_of(ci * L, L), L)
        r, k, v, ld, a = r_ref[sl, :], k_ref[sl, :], v_ref[sl, :], ld_ref[sl, :], a_ref[sl, :]
        kk = k * kk_ref[...]
        kk = kk / jnp.maximum(jnp.sqrt(_dot(kk * kk, head_ones, hp)), 1e-12)
        k2 = modified_key(k, a)
        a_vec, b_vec = -kk, kk * a
        c = _dot(cum_rows, ld, hp)
        to_end = jnp.exp(c[L - 1:L, :] - c)
        inv = jnp.exp(-c)
        a_st = stack(a_vec * jnp.exp(c - ld))
        r_st = stack(r * jnp.exp(c))
        b_st = stack(b_vec * inv)
        k_st = stack(k2 * inv)
        v_st = stack(v)
        m_ab = jnp.where(lower_strict, _dot_nt(a_st, b_st, hp), 0.0)
        m_ak = jnp.where(lower_strict, _dot_nt(a_st, k_st, hp), 0.0)
        a_rb = jnp.where(lower_incl, _dot_nt(r_st, b_st, hp), 0.0)
        a_rk = jnp.where(lower_incl, _dot_nt(r_st, k_st, hp), 0.0)
        t_inv = eye
        w = 1
        while w < L:
            below = jnp.logical_and((row // (2 * w)) == (col // (2 * w)),
                                    jnp.logical_and((row % (2 * w)) >= w, (col % (2 * w)) < w))
            t_inv = t_inv + _dot(_dot(t_inv, jnp.where(below, m_ab, 0.0), hp), t_inv, hp)
            w *= 2
        s = s_ref[...]
        u = _dot(t_inv, _dot(a_st, s, hp) + _dot(m_ak, v_st, hp), hp)
        o_st = _dot(r_st, s, hp) + _dot(a_rb, u, hp) + _dot(a_rk, v_st, hp)
        o_ref[sl, :] = o_st[:L] + o_st[L:]
        decay_col = jnp.exp(_dot_tn(ld, ones_ln, hp))
        s_ref[...] = (decay_col * s + _dot_tn(stack(b_vec * to_end), u, hp)
                      + _dot_tn(stack(k2 * to_end), v_st, hp))
        return carry

    lax.fori_loop(0, n_chunks, chunk_step, 0)

    o = o_ref[...]
    inv_n = 1.0 / HEAD_DIM
    mu = _dot(o, head_ones, hp) * inv_n
    d = o - mu
    var = _dot(d * d, head_ones, hp) * inv_n
    r, v = r_ref[...], v_ref[...]
    k2 = modified_key(k_ref[...], a_ref[...])
    bonus = _dot(r * k2 * rk_ref[...], head_ones, hp) * v
    o_ref[...] = d * lax.rsqrt(var + GN_EPS) * lg_ref[...] + lb_ref[...] + bonus


def _rwkv_scan(r, k, v, ld, a, seq_len, k_k, k_a, r_k, lnx_g, lnx_b):
    n, d = r.shape
    batch = n // seq_len
    pairs = d // LANES
    rows = min(SCAN_ROWS, seq_len)
    steps = seq_len // rows
    seq = pl.BlockSpec((rows, LANES), lambda p, c: ((p // pairs) * steps + c, p % pairs))
    par = pl.BlockSpec((1, LANES), lambda p, c: (0, p % pairs))
    return pl.pallas_call(
        functools.partial(_rwkv_scan_kernel, chunk=SCAN_CHUNK, n_chunks=rows // SCAN_CHUNK),
        grid=(batch * pairs, steps),
        in_specs=[seq] * 5 + [par] * 5,
        out_specs=seq,
        out_shape=jax.ShapeDtypeStruct((n, d), F32),
        scratch_shapes=[pltpu.VMEM((LANES, LANES), F32)],
        compiler_params=_cparams(("parallel", "arbitrary")),
        name="rwkv_scan",
    )(r, k, v, ld, a, k_k.reshape(1, d), k_a.reshape(1, d), r_k.reshape(1, d),
      lnx_g.reshape(1, d), lnx_b.reshape(1, d))


def _proj_ln_kernel(*refs, gated):
    if gated:
        x_ref, o_ref, g_ref, w_ref, lg_ref, lb_ref, out_ref = refs
        h = o_ref[...] * g_ref[...]
    else:
        x_ref, o_ref, w_ref, lg_ref, lb_ref, out_ref = refs
        h = o_ref[...]
    y = ALPHA * x_ref[...] + _dot(h.astype(BF16), w_ref[...])
    out_ref[...] = _layer_norm(y, lg_ref[...], lb_ref[...])


def _proj_ln(xf, o, g, w_o, ln_g, ln_b):
    n, d = xf.shape
    tm = min(ROW_TILE, n)
    row_tile = pl.BlockSpec((tm, d), lambda i: (i, 0))
    full = lambda shape: pl.BlockSpec(shape, lambda i: (0,) * len(shape))
    gated = g is not None
    acts = (xf, o, g) if gated else (xf, o)
    return pl.pallas_call(
        functools.partial(_proj_ln_kernel, gated=gated),
        grid=(n // tm,),
        in_specs=[row_tile] * len(acts) + [full((d, d)), full((1, d)), full((1, d))],
        out_specs=row_tile,
        out_shape=jax.ShapeDtypeStruct((n, d), F32),
        compiler_params=_cparams(("parallel",)),
        name="proj_ln",
    )(*acts, w_o.astype(BF16), ln_g.reshape(1, d), ln_b.reshape(1, d))


def _sb_qkv_kernel(x_ref, w_ref, q_ref, k_ref, v_ref):
    d = x_ref.shape[1]
    qkv = _dot(x_ref[...].astype(BF16), w_ref[...])
    q_ref[...] = (qkv[:, :d] * (HEAD_DIM ** -0.5)).astype(BF16)
    k_ref[...] = qkv[:, d:2 * d].astype(BF16)
    v_ref[...] = qkv[:, 2 * d:].astype(BF16)


def _sb_qkv(xf, w_qkv):
    n, d = xf.shape
    tm = min(ROW_TILE, n)
    row_tile = pl.BlockSpec((tm, d), lambda i: (i, 0))
    out = jax.ShapeDtypeStruct((n, d), BF16)
    return pl.pallas_call(
        _sb_qkv_kernel,
        grid=(n // tm,),
        in_specs=[row_tile, pl.BlockSpec((d, 3 * d), lambda i: (0, 0))],
        out_specs=[row_tile] * 3,
        out_shape=[out] * 3,
        compiler_params=_cparams(("parallel",)),
        name="sb_qkv",
    )(xf, w_qkv.astype(BF16))


def _sb_attn_kernel(q_ref, k_ref, v_ref, o_ref, acc_ref, run_ref, *, blk):
    i = pl.program_id(1)
    head0 = lax.broadcasted_iota(jnp.int32, (1, LANES), 1) < HEAD_DIM
    q = q_ref[...]
    zero = jnp.zeros_like(q)
    q_heads = (jnp.where(head0, q, zero), jnp.where(head0, zero, q))
    kr = lax.broadcasted_iota(jnp.int32, (blk, 2 * blk), 0)
    kc = lax.broadcasted_iota(jnp.int32, (blk, 2 * blk), 1)
    suffix = jnp.logical_or(kr > kc, kc >= blk).astype(BF16)
    causal = lax.broadcasted_iota(jnp.int32, (blk, blk), 1) < lax.broadcasted_iota(jnp.int32, (blk, blk), 0)
    acc_ref[...] = jnp.zeros_like(acc_ref)
    run_ref[...] = jnp.zeros_like(run_ref)

    def tile(j, masked):
        ks = pl.ds(pl.multiple_of(j * blk, blk), blk)
        kb, vb = k_ref[ks, :], v_ref[ks, :]
        for h in range(2):
            z = _dot_nt(q_heads[h], kb)
            log_keep = -_softplus(z)
            if masked:
                log_keep = jnp.where(causal, log_keep, 0.0)
            hi = log_keep.astype(BF16)
            lo = (log_keep - hi.astype(F32)).astype(BF16)
            sums = _dot(hi, suffix) + _dot(lo, suffix)
            att = jnp.exp(z + log_keep + run_ref[h] + sums[:, :blk])
            if masked:
                att = jnp.where(causal, att, 0.0)
            acc_ref[h] += _dot(att.astype(BF16), vb)
            run_ref[h] += sums[:, blk:]

    tile(i, True)

    def body(step, carry):
        tile(i - 1 - step, False)
        return carry

    lax.fori_loop(0, i, body, 0)
    o_ref[...] = jnp.where(head0, acc_ref[0], acc_ref[1])


def _sb_attn(q, k, v, seq_len):
    n, d = q.shape
    batch = n // seq_len
    pairs = d // LANES
    blk = min(SB_BLOCK, seq_len)
    nq = seq_len // blk
    q_spec = pl.BlockSpec((blk, LANES), lambda p, i: ((p // pairs) * nq + i, p % pairs))
    kv_spec = pl.BlockSpec((seq_len, LANES), lambda p, i: (p // pairs, p % pairs))
    return pl.pallas_call(
        functools.partial(_sb_attn_kernel, blk=blk),
        grid=(batch * pairs, nq),
        in_specs=[q_spec, kv_spec, kv_spec],
        out_specs=q_spec,
        out_shape=jax.ShapeDtypeStruct((n, d), F32),
        scratch_shapes=[pltpu.VMEM((2, blk, LANES), F32), pltpu.VMEM((2, blk, LANES), F32)],
        compiler_params=_cparams(("parallel", "arbitrary")),
        name="sb_attn",
    )(q, k, v)


def _router_kernel(x_ref, w_ref, b_ref, idx_ref, gate_ref):
    logits = _dot_nt(w_ref[...], x_ref[...], HIGHEST) + b_ref[...]
    n_exp = logits.shape[0]
    e_idx = lax.broadcasted_iota(jnp.int32, logits.shape, 0)
    vals, idxs = [], []
    work = logits
    for _ in range(TOP_K):
        m = jnp.max(work, axis=0, keepdims=True)
        am = jnp.min(jnp.where(work == m, e_idx, n_exp), axis=0, keepdims=True)
        vals.append(m)
        idxs.append(am)
        work = jnp.where(e_idx == am, -jnp.inf, work)
    ex = [jnp.exp(vj - vals[0]) for vj in vals]
    inv = 1.0 / (ex[0] + ex[1] + ex[2] + ex[3])
    pad_i = jnp.zeros((8 - TOP_K, logits.shape[1]), jnp.int32)
    pad_f = jnp.zeros((8 - TOP_K, logits.shape[1]), F32)
    idx_ref[...] = jnp.concatenate(idxs + [pad_i], axis=0)
    gate_ref[...] = jnp.concatenate([e * inv for e in ex] + [pad_f], axis=0)


def _router(xf, w_router, b_router):
    n, d = xf.shape
    n_exp = w_router.shape[1]
    tm = min(ROW_TILE, n)
    out_spec = pl.BlockSpec((8, tm), lambda i: (0, i))
    idx, gate = pl.pallas_call(
        _router_kernel,
        grid=(n // tm,),
        in_specs=[pl.BlockSpec((tm, d), lambda i: (i, 0)), pl.BlockSpec((n_exp, d), lambda i: (0, 0)),
                  pl.BlockSpec((n_exp, 1), lambda i: (0, 0))],
        out_specs=[out_spec, out_spec],
        out_shape=[jax.ShapeDtypeStruct((8, n), jnp.int32), jax.ShapeDtypeStruct((8, n), F32)],
        compiler_params=_cparams(("parallel",)),
        name="moe_router",
    )(xf, w_router.T, b_router.reshape(n_exp, 1))
    return idx[:TOP_K].T, gate[:TOP_K].T


def _group_rows(top_idx, n_exp, rows):
    n = top_idx.shape[0]
    n_assign = n * TOP_K
    expert = top_idx.reshape(-1)
    order = jnp.argsort(expert, stable=True).astype(jnp.int32)
    counts = jnp.sum(expert[:, None] == jnp.arange(n_exp, dtype=jnp.int32)[None, :], axis=0, dtype=jnp.int32)
    padded = (counts + rows - 1) // rows * rows
    pad_end = jnp.cumsum(padded)
    pad_start = pad_end - padded
    start = jnp.cumsum(counts) - counts
    n_blocks = n_assign // rows + n_exp
    block_expert = jnp.minimum(
        jnp.searchsorted(pad_end, jnp.arange(n_blocks, dtype=jnp.int32) * rows, side="right"),
        n_exp - 1).astype(jnp.int32)
    pos = jnp.arange(n_blocks * rows, dtype=jnp.int32)
    blk = pos // rows
    e_p = block_expert[blk]
    off = pos - pad_start[e_p]
    valid = off < counts[e_p]
    assign = order[jnp.clip(start[e_p] + off, 0, n_assign - 1)]
    tok = assign // TOP_K
    slot = assign % TOP_K
    src = jnp.where(valid, tok, 0)
    dst = jnp.where(valid, slot * n + tok, 0)
    n_valid = jnp.sum(valid.reshape(n_blocks, rows), axis=1, dtype=jnp.int32)
    n_used = (pad_end[-1] // rows).astype(jnp.int32).reshape(1)
    return (src.reshape(n_blocks, 1, rows), dst.reshape(n_blocks, 1, rows), block_expert, n_valid, n_used)


def _expert_kernel(bexp_ref, nvalid_ref, nused_ref, src_ref, src_next_ref, dst_ref, x_hbm, w1g_ref, w1l_ref,
                   b1g_ref, b1l_ref, w2_ref, b2_ref, y_hbm, xbuf, ybuf, gsem, ssem, *, rows):
    del bexp_ref
    b = pl.program_id(0)
    n_steps = pl.num_programs(0)
    n_used = nused_ref[0]
    slot = b % 2

    def gather(idx_ref, s):
        def body(i, carry):
            pltpu.make_async_copy(x_hbm.at[pl.ds(idx_ref[0, 0, i], 1), :], xbuf.at[s, pl.ds(i, 1), :],
                                  gsem.at[s]).start()
            return carry
        lax.fori_loop(0, rows, body, 0)

    def wait_gather(s):
        pltpu.make_async_copy(x_hbm.at[pl.ds(0, rows), :], xbuf.at[s], gsem.at[s]).wait()

    def wait_scatter(s, blk):
        def body(i, carry):
            pltpu.make_async_copy(ybuf.at[s, pl.ds(0, 1), :], y_hbm.at[pl.ds(0, 1), :], ssem.at[s]).wait()
            return carry
        lax.fori_loop(0, nvalid_ref[blk], body, 0)

    @pl.when(jnp.logical_and(b == 0, n_used > 0))
    def _():
        gather(src_ref, 0)

    @pl.when(b + 1 < n_used)
    def _():
        gather(src_next_ref, 1 - slot)

    @pl.when(jnp.logical_and(b >= 2, b - 2 < n_used))
    def _():
        wait_scatter(slot, b - 2)

    @pl.when(b < n_used)
    def _():
        wait_gather(slot)
        xb = xbuf[slot].astype(BF16)
        glu = jnp.minimum(_dot(xb, w1g_ref[0]) + b1g_ref[0], SWIGLU_LIMIT)
        lin = jnp.clip(_dot(xb, w1l_ref[0]) + b1l_ref[0], -SWIGLU_LIMIT, SWIGLU_LIMIT)
        act = glu * _sigmoid(SWIGLU_ALPHA * glu) * (lin + 1.0)
        ybuf[slot] = _dot(act.astype(BF16), w2_ref[0]) + b2_ref[0]

        def body(i, carry):
            pltpu.make_async_copy(ybuf.at[slot, pl.ds(i, 1), :], y_hbm.at[pl.ds(dst_ref[0, 0, i], 1), :],
                                  ssem.at[slot]).start()
            return carry
        lax.fori_loop(0, nvalid_ref[b], body, 0)

    @pl.when(b == n_steps - 1)
    def _():
        @pl.when(jnp.logical_and(b >= 1, b - 1 < n_used))
        def _():
            wait_scatter(1 - slot, b - 1)

        @pl.when(b < n_used)
        def _():
            wait_scatter(slot, b)


def _moe_experts(xf, src, dst, block_expert, n_valid, n_used, w1g, w1l, b1g, b1l, w2, b2):
    n, d = xf.shape
    n_blocks, _, rows = src.shape
    n_exp, _, f = w1g.shape
    idx_spec = lambda shift: pl.BlockSpec(
        (1, 1, rows), lambda b, be, nv, nu: (jnp.minimum(b + shift, n_blocks - 1), 0, 0), memory_space=pltpu.SMEM)
    per_expert = lambda shape: pl.BlockSpec((1,) + shape, lambda b, be, nv, nu: (be[b], 0, 0))
    grid_spec = pltpu.PrefetchScalarGridSpec(
        num_scalar_prefetch=3,
        grid=(n_blocks,),
        in_specs=[idx_spec(0), idx_spec(1), idx_spec(0), pl.BlockSpec(memory_space=pl.ANY),
                  per_expert((d, f)), per_expert((d, f)), per_expert((1, f)), per_expert((1, f)),
                  per_expert((f, d)), per_expert((1, d))],
        out_specs=pl.BlockSpec(memory_space=pl.ANY),
        scratch_shapes=[pltpu.VMEM((2, rows, d), F32), pltpu.VMEM((2, rows, d), F32),
                        pltpu.SemaphoreType.DMA((2,)), pltpu.SemaphoreType.DMA((2,))],
    )
    return pl.pallas_call(
        functools.partial(_expert_kernel, rows=rows),
        grid_spec=grid_spec,
        out_shape=jax.ShapeDtypeStruct((TOP_K * n, d), F32),
        compiler_params=_cparams(("arbitrary",)),
        name="moe_experts",
    )(block_expert, n_valid, n_used, src, src, dst, xf, w1g, w1l, b1g, b1l, w2, b2)


def _combine_kernel(x_ref, gate_ref, y0_ref, y1_ref, y2_ref, y3_ref, lg_ref, lb_ref, out_ref):
    gate = gate_ref[...]
    f = (gate[:, 0:1] * y0_ref[...] + gate[:, 1:2] * y1_ref[...]
         + gate[:, 2:3] * y2_ref[...] + gate[:, 3:4] * y3_ref[...])
    out_ref[...] = _layer_norm(ALPHA * x_ref[...] + f, lg_ref[...], lb_ref[...])


def _moe_combine(xf, gate, y, ln_g, ln_b):
    n, d = xf.shape
    tm = min(ROW_TILE, n)
    tiles = n // tm
    row_tile = pl.BlockSpec((tm, d), lambda i: (i, 0))
    slot_tile = lambda j: pl.BlockSpec((tm, d), lambda i: (j * tiles + i, 0))
    vec = pl.BlockSpec((1, d), lambda i: (0, 0))
    return pl.pallas_call(
        _combine_kernel,
        grid=(tiles,),
        in_specs=[row_tile, pl.BlockSpec((tm, TOP_K), lambda i: (i, 0))] + [slot_tile(j) for j in range(TOP_K)]
                 + [vec, vec],
        out_specs=row_tile,
        out_shape=jax.ShapeDtypeStruct((n, d), F32),
        compiler_params=_cparams(("parallel",)),
        name="moe_combine",
    )(xf, gate, y, y, y, y, ln_g.reshape(1, d), ln_b.reshape(1, d))


def _moe(xf, w_router, b_router, w1, b1, w2, b2, ln_g, ln_b):
    n, d = xf.shape
    n_exp = w_router.shape[1]
    rows = min(EXPERT_ROWS, n)
    top_idx, gate = _router(xf, w_router, b_router)
    src, dst, block_expert, n_valid, n_used = _group_rows(top_idx, n_exp, rows)
    w1g, w1l = w1[:, :, 0::2].astype(BF16), w1[:, :, 1::2].astype(BF16)
    b1g, b1l = b1[:, None, 0::2], b1[:, None, 1::2]
    y = _moe_experts(xf, src, dst, block_expert, n_valid, n_used, w1g, w1l, b1g, b1l, w2.astype(BF16),
                     b2[:, None, :])
    return _moe_combine(xf, gate, y, ln_g, ln_b)


def kernel(x, rw_mix, rw_w_rkv, rw_w0, rw_w1, rw_w2, rw_a0, rw_a1, rw_a2, rw_g1, rw_g2, rw_k_k, rw_k_a, rw_r_k,
           rw_lnx_g, rw_lnx_b, rw_w_o, sb_w_qkv, sb_w_o, moe_w_router, moe_b_router, moe_w1, moe_b1, moe_w2,
           moe_b2, ln_g, ln_b):
    batch, seq_len, d = x.shape
    xf = x.reshape(batch * seq_len, d)
    for i in range(DEPTH):
        j = i // 2
        if i % 2 == 0:
            r, k, v, ld, a, g = _rwkv_proj(xf, seq_len, rw_mix[j], rw_w_rkv[j], rw_w0[j], rw_w1[j], rw_w2[j],
                                           rw_a0[j], rw_a1[j], rw_a2[j], rw_g1[j], rw_g2[j])
            o = _rwkv_scan(r, k, v, ld, a, seq_len, rw_k_k[j], rw_k_a[j], rw_r_k[j], rw_lnx_g[j], rw_lnx_b[j])
            xf = _proj_ln(xf, o, g, rw_w_o[j], ln_g[i, 0], ln_b[i, 0])
        else:
            q, k, v = _sb_qkv(xf, sb_w_qkv[j])
            o = _sb_attn(q, k, v, seq_len)
            xf = _proj_ln(xf, o, None, sb_w_o[j], ln_g[i, 0], ln_b[i, 0])
        xf = _moe(xf, moe_w_router[i], moe_b_router[i], moe_w1[i], moe_b1[i], moe_w2[i], moe_b2[i],
                  ln_g[i, 1], ln_b[i, 1])
    return xf.reshape(batch, seq_len, d)
```

```python
import functools

import jax
import jax.numpy as jnp
from jax import lax
from jax.experimental import pallas as pl
from jax.experimental.pallas import tpu as pltpu

F32 = jnp.float32
BF16 = jnp.bfloat16
HIGHEST = lax.Precision.HIGHEST

DEPTH = 2
ALPHA = (2 * DEPTH) ** 0.25
LN_EPS = 1e-5
GN_EPS = 64e-5
HEAD_DIM = 64
LANES = 128
TOP_K = 4
SWIGLU_LIMIT = 7.0
SWIGLU_ALPHA = 1.702

ROW_TILE = 256
SCAN_CHUNK = 64
SCAN_ROWS = 512
SCAN_UNROLL = 8
SB_Q_BLOCK = 512
SB_K_BLOCK = LANES
SB_GROUP = 4
SB_ROW_CHUNK = 64
EXPERT_ROWS = 256
DMA_UNROLL = 8
SPLIT_COLS = 512
VMEM_LIMIT = 56 * 1024 * 1024


def _cparams(semantics):
    return pltpu.CompilerParams(dimension_semantics=semantics, vmem_limit_bytes=VMEM_LIMIT)


def _dot(a, b, precision=None):
    return jnp.dot(a, b, preferred_element_type=F32, precision=precision)


def _dot_nt(a, b, precision=None):
    return lax.dot_general(a, b, (((1,), (1,)), ((), ())), preferred_element_type=F32, precision=precision)


def _dot_tn(a, b, precision=None):
    return lax.dot_general(a, b, (((0,), (0,)), ((), ())), preferred_element_type=F32, precision=precision)


def _split2(a):
    hi = a.astype(BF16)
    return hi, (a - hi.astype(F32)).astype(BF16)


def _split3(a):
    hi = a.astype(BF16)
    rest = a - hi.astype(F32)
    mid = rest.astype(BF16)
    return hi, mid, (rest - mid.astype(F32)).astype(BF16)


def _mm3(a, b, dot=_dot):
    return dot(a[0], b[0]) + (dot(a[0], b[1]) + dot(a[1], b[0]))


def _softplus(u):
    return jnp.maximum(u, 0.0) + jnp.log(1.0 + jnp.exp(-jnp.abs(u)))


def _sigmoid(u):
    return 1.0 / (1.0 + jnp.exp(-u))


def _layer_norm(y, g, b):
    mu = jnp.mean(y, axis=-1, keepdims=True)
    d = y - mu
    var = jnp.mean(d * d, axis=-1, keepdims=True)
    return d * lax.rsqrt(var + LN_EPS) * g + b


def _head_block_ones():
    r = lax.broadcasted_iota(jnp.int32, (LANES, LANES), 0) // HEAD_DIM
    c = lax.broadcasted_iota(jnp.int32, (LANES, LANES), 1) // HEAD_DIM
    return (r == c).astype(BF16)


def _head_sums(x, head_ones):
    m = x.shape[0]
    s = _dot(jnp.concatenate(_split3(x), axis=0), head_ones)
    return s[:m] + s[m:2 * m] + s[2 * m:]


def _rwkv_proj_kernel(x_ref, xp_ref, mix_ref, wrkv_ref, w0_ref, w1_ref, w2_ref, a0_ref, a1_ref, a2_ref,
                      g1_ref, g2_ref, r_ref, k_ref, v_ref, ld_ref, a_ref, g_ref, *, tiles_per_seq):
    i = pl.program_id(0)
    x = x_ref[...]
    prev_last = jnp.where(i % tiles_per_seq == 0, 0.0, xp_ref[7:8, :])
    row = lax.broadcasted_iota(jnp.int32, x.shape, 0)
    x_prev = jnp.where(row == 0, prev_last, pltpu.roll(x, 1, 0))
    xx = x_prev - x

    def mixed(j):
        return (x + xx * mix_ref[j:j + 1, :]).astype(BF16)

    r_ref[...] = _dot(mixed(0), wrkv_ref[0])
    w = w0_ref[...] + _dot(jnp.tanh(_dot(mixed(1), w1_ref[...])).astype(BF16), w2_ref[...])
    ld_ref[...] = -jnp.exp(-_softplus(-w) - 0.5)
    k_ref[...] = _dot(mixed(2), wrkv_ref[1])
    v_ref[...] = _dot(mixed(3), wrkv_ref[2])
    a_ref[...] = _sigmoid(a0_ref[...] + _dot(_dot(mixed(4), a1_ref[...]).astype(BF16), a2_ref[...]))
    g_ref[...] = _dot(_sigmoid(_dot(mixed(5), g1_ref[...])).astype(BF16), g2_ref[...])


def _rwkv_proj(xf, seq_len, mix, w_rkv, w0, w1, w2, a0, a1, a2, g1, g2):
    n, d = xf.shape
    tm = min(ROW_TILE, seq_len)
    full = lambda shape: pl.BlockSpec(shape, lambda i: (0,) * len(shape))
    row_tile = pl.BlockSpec((tm, d), lambda i: (i, 0))
    prev_rows = pl.BlockSpec((8, d), lambda i: (jnp.maximum(i * (tm // 8) - 1, 0), 0))
    mix8 = jnp.concatenate([mix, jnp.zeros((2, d), F32)], axis=0)
    out = jax.ShapeDtypeStruct((n, d), F32)
    return pl.pallas_call(
        functools.partial(_rwkv_proj_kernel, tiles_per_seq=seq_len // tm),
        grid=(n // tm,),
        in_specs=[row_tile, prev_rows, full((8, d)), full(w_rkv.shape), full((1, d)), full(w1.shape),
                  full(w2.shape), full((1, d)), full(a1.shape), full(a2.shape), full(g1.shape), full(g2.shape)],
        out_specs=[row_tile] * 6,
        out_shape=[out] * 6,
        compiler_params=_cparams(("parallel",)),
        name="rwkv_proj",
    )(xf, xf, mix8, w_rkv.astype(BF16), w0.reshape(1, d), w1.astype(BF16), w2.astype(BF16),
      a0.reshape(1, d), a1.astype(BF16), a2.astype(BF16), g1.astype(BF16), g2.astype(BF16))


def _rwkv_scan_kernel(r_ref, k_ref, v_ref, ld_ref, a_ref, kk_ref, ka_ref, rk_ref, lg_ref, lb_ref,
                      o_ref, s_ref, rhat_ref, o0_ref, p_ref, q_ref, dcol_ref, *, chunk, n_chunks, unroll):
    L = chunk

    @pl.when(pl.program_id(1) == 0)
    def _():
        s_ref[...] = jnp.zeros_like(s_ref)

    head0 = lax.broadcasted_iota(jnp.int32, (1, LANES), 1) < HEAD_DIM
    row = lax.broadcasted_iota(jnp.int32, (2 * L, 2 * L), 0)
    col = lax.broadcasted_iota(jnp.int32, (2 * L, 2 * L), 1)
    same_head = (row // L) == (col // L)
    lower_strict = jnp.logical_and(same_head, col < row)
    lower_incl = jnp.logical_and(same_head, col <= row)
    eye = (row == col).astype(F32)
    cum_rows = (lax.broadcasted_iota(jnp.int32, (L, L), 1)
                <= lax.broadcasted_iota(jnp.int32, (L, L), 0)).astype(BF16)
    head_ones = _head_block_ones()
    ones_ln = jnp.ones((L, LANES), BF16)

    def stack(t):
        return jnp.concatenate([jnp.where(head0, t, 0.0), jnp.where(head0, 0.0, t)], axis=0)

    def modified_key(k, a):
        return k * (1.0 + (a - 1.0) * ka_ref[...])

    def prepare(ci):
        sl = pl.ds(pl.multiple_of(ci * L, L), L)
        r, k, v, ld, a = r_ref[sl, :], k_ref[sl, :], v_ref[sl, :], ld_ref[sl, :], a_ref[sl, :]
        kk = k * kk_ref[...]
        kk_sq = _head_sums(kk * kk, head_ones)
        ld3 = jnp.concatenate(_split3(ld), axis=1)
        c3 = _dot(cum_rows, ld3)
        tot3 = _dot_tn(ld3, ones_ln)
        yield
        kk = kk / jnp.maximum(jnp.sqrt(kk_sq), 1e-12)
        k2 = modified_key(k, a)
        a_vec, b_vec = -kk, kk * a
        c = c3[:, :LANES] + c3[:, LANES:2 * LANES] + c3[:, 2 * LANES:]
        dcol_ref[ci] = jnp.exp(tot3[:LANES] + tot3[LANES:2 * LANES] + tot3[2 * LANES:])
        to_end = jnp.exp(c[L - 1:L, :] - c)
        inv = jnp.exp(-c)
        a_st = stack(a_vec * jnp.exp(c - ld))
        r_st = stack(r * jnp.exp(c))
        v_sp = _split2(stack(v))
        bh_sp = _split2(stack(b_vec * to_end))
        kh_sp = _split2(stack(k2 * to_end))
        ar = _split2(jnp.concatenate([a_st, r_st], axis=0))
        bk = _split2(jnp.concatenate([stack(b_vec * inv), stack(k2 * inv)], axis=0))
        prod = _mm3(ar, bk, _dot_nt)
        kv = _mm3(kh_sp, v_sp, _dot_tn)
        yield
        m_ab = jnp.where(lower_strict, prod[:2 * L, :2 * L], 0.0)
        m_ak = jnp.where(lower_strict, prod[:2 * L, 2 * L:], 0.0)
        a_rb = jnp.where(lower_incl, prod[2 * L:, :2 * L], 0.0)
        a_rk = jnp.where(lower_incl, prod[2 * L:, 2 * L:], 0.0)
        w0 = _mm3(_split2(m_ak), v_sp)
        rkv = _mm3(_split2(a_rk), v_sp)
        t_inv = eye
        w = 1
        while w < L:
            below = jnp.logical_and((row // (2 * w)) == (col // (2 * w)),
                                    jnp.logical_and((row % (2 * w)) >= w, (col % (2 * w)) < w))
            t_sp = _split2(t_inv)
            tm = _mm3(t_sp, _split2(jnp.where(below, m_ab, 0.0)))
            yield
            tmt = _mm3(_split2(tm), t_sp)
            yield
            t_inv = t_inv + tmt
            w *= 2
        au = _mm3(_split2(t_inv), _split2(jnp.concatenate([a_st, w0], axis=1)))
        yield
        au_sp = _split2(au)
        ro = _mm3(_split2(a_rb), au_sp)
        pq = _mm3(bh_sp, au_sp, _dot_tn)
        yield
        rhat_ref[ci] = r_st + ro[:, :LANES]
        o0_ref[ci] = ro[:, LANES:] + rkv
        p_ref[ci] = pq[:, :LANES]
        q_ref[ci] = pq[:, LANES:] + kv

    def prepare_group(gi, carry):
        chunks = [prepare(gi * unroll + u) for u in range(unroll)]
        while chunks:
            for g in chunks:
                if next(g, StopIteration) is StopIteration:
                    chunks = []
        return carry

    lax.fori_loop(0, n_chunks // unroll, prepare_group, 0)

    s = s_ref[...]
    for ci in range(n_chunks):
        s_sp = _split2(s)
        o_st = _mm3(_split2(rhat_ref[ci]), s_sp) + o0_ref[ci]
        o_ref[ci * L:(ci + 1) * L, :] = o_st[:L] + o_st[L:]
        s = dcol_ref[ci] * s + _mm3(_split2(p_ref[ci]), s_sp) + q_ref[ci]
    s_ref[...] = s

    o = o_ref[...]
    inv_n = 1.0 / HEAD_DIM
    mu = _head_sums(o, head_ones) * inv_n
    d = o - mu
    var = _head_sums(d * d, head_ones) * inv_n
    r, v = r_ref[...], v_ref[...]
    k2 = modified_key(k_ref[...], a_ref[...])
    bonus = _head_sums(r * k2 * rk_ref[...], head_ones) * v
    o_ref[...] = d * lax.rsqrt(var + GN_EPS) * lg_ref[...] + lb_ref[...] + bonus


def _rwkv_scan(r, k, v, ld, a, seq_len, k_k, k_a, r_k, lnx_g, lnx_b):
    n, d = r.shape
    batch = n // seq_len
    pairs = d // LANES
    rows = min(SCAN_ROWS, seq_len)
    steps = seq_len // rows
    n_chunks = rows // SCAN_CHUNK
    seq = pl.BlockSpec((rows, LANES), lambda p, c: ((p // pairs) * steps + c, p % pairs))
    par = pl.BlockSpec((1, LANES), lambda p, c: (0, p % pairs))
    per_chunk = pltpu.VMEM((n_chunks, 2 * SCAN_CHUNK, LANES), F32)
    return pl.pallas_call(
        functools.partial(_rwkv_scan_kernel, chunk=SCAN_CHUNK, n_chunks=n_chunks,
                          unroll=min(SCAN_UNROLL, n_chunks)),
        grid=(batch * pairs, steps),
        in_specs=[seq] * 5 + [par] * 5,
        out_specs=seq,
        out_shape=jax.ShapeDtypeStruct((n, d), F32),
        scratch_shapes=[pltpu.VMEM((LANES, LANES), F32)] + [per_chunk] * 5,
        compiler_params=_cparams(("parallel", "arbitrary")),
        name="rwkv_scan",
    )(r, k, v, ld, a, k_k.reshape(1, d), k_a.reshape(1, d), r_k.reshape(1, d),
      lnx_g.reshape(1, d), lnx_b.reshape(1, d))


def _proj_ln_kernel(*refs, gated):
    if gated:
        x_ref, o_ref, g_ref, w_ref, lg_ref, lb_ref, out_ref = refs
        h = o_ref[...] * g_ref[...]
    else:
        x_ref, o_ref, w_ref, lg_ref, lb_ref, out_ref = refs
        h = o_ref[...]
    y = ALPHA * x_ref[...] + _dot(h.astype(BF16), w_ref[...])
    out_ref[...] = _layer_norm(y, lg_ref[...], lb_ref[...])


def _proj_ln(xf, o, g, w_o, ln_g, ln_b):
    n, d = xf.shape
    tm = min(ROW_TILE, n)
    row_tile = pl.BlockSpec((tm, d), lambda i: (i, 0))
    full = lambda shape: pl.BlockSpec(shape, lambda i: (0,) * len(shape))
    gated = g is not None
    acts = (xf, o, g) if gated else (xf, o)
    return pl.pallas_call(
        functools.partial(_proj_ln_kernel, gated=gated),
        grid=(n // tm,),
        in_specs=[row_tile] * len(acts) + [full((d, d)), full((1, d)), full((1, d))],
        out_specs=row_tile,
        out_shape=jax.ShapeDtypeStruct((n, d), F32),
        compiler_params=_cparams(("parallel",)),
        name="proj_ln",
    )(*acts, w_o.astype(BF16), ln_g.reshape(1, d), ln_b.reshape(1, d))


def _sb_qkv_kernel(x_ref, w_ref, q_ref, k_ref, v_ref):
    d = x_ref.shape[1]
    qkv = _dot(x_ref[...].astype(BF16), w_ref[...])
    q_ref[...] = (qkv[:, :d] * (HEAD_DIM ** -0.5)).astype(BF16)
    k_ref[...] = qkv[:, d:2 * d].astype(BF16)
    v_ref[...] = qkv[:, 2 * d:].astype(BF16)


def _sb_qkv(xf, w_qkv):
    n, d = xf.shape
    tm = min(ROW_TILE, n)
    row_tile = pl.BlockSpec((tm, d), lambda i: (i, 0))
    out = jax.ShapeDtypeStruct((n, d), BF16)
    return pl.pallas_call(
        _sb_qkv_kernel,
        grid=(n // tm,),
        in_specs=[row_tile, pl.BlockSpec((d, 3 * d), lambda i: (0, 0))],
        out_specs=[row_tile] * 3,
        out_shape=[out] * 3,
        compiler_params=_cparams(("parallel",)),
        name="sb_qkv",
    )(xf, w_qkv.astype(BF16))


def _sb_attn_kernel(q_ref, k_ref, v_ref, o_ref, acc_ref, run_ref, ls_buf, lk_buf, sum_buf, att_buf,
                    *, bq, bk, group):
    i = pl.program_id(1)
    ratio = bq // bk
    m = 2 * bq
    head0 = lax.broadcasted_iota(jnp.int32, (1, LANES), 1) < HEAD_DIM
    q = q_ref[...]
    zero = jnp.zeros_like(q)
    q2 = jnp.concatenate([jnp.where(head0, q, zero), jnp.where(head0, zero, q)], axis=0)
    kr = lax.broadcasted_iota(jnp.int32, (2 * bk, 2 * bk), 0) % bk
    kc = lax.broadcasted_iota(jnp.int32, (2 * bk, 2 * bk), 1)
    suffix = jnp.logical_or(kr > kc, kc >= bk).astype(BF16)
    acc_ref[...] = jnp.zeros_like(acc_ref)
    run_ref[...] = jnp.zeros_like(run_ref)
    row_chunks = [slice(c, c + SB_ROW_CHUNK) for c in range(0, m, SB_ROW_CHUNK)]

    def causal(j, rs):
        q_pos = i * bq + (lax.broadcasted_iota(jnp.int32, (SB_ROW_CHUNK, bk), 0) + rs.start) % bq
        return (j * bk + lax.broadcasted_iota(jnp.int32, (SB_ROW_CHUNK, bk), 1)) < q_pos

    def stages(j, masked, u):
        ks = pl.ds(pl.multiple_of(j * bk, bk), bk)

        def scores():
            ls_buf[u] = _dot_nt(q2, k_ref[ks, :])

        def log_terms():
            for rs in row_chunks:
                z = ls_buf[u, rs, :]
                log_keep = -_softplus(z)
                if masked:
                    log_keep = jnp.where(causal(j, rs), log_keep, 0.0)
                hi, lo = _split2(log_keep)
                lk_buf[u, rs, :bk] = hi
                lk_buf[u, rs, bk:] = lo
                ls_buf[u, rs, :] = z + log_keep

        def suffix_sums():
            sum_buf[u] = _dot(lk_buf[u], suffix)

        def weights():
            for rs in row_chunks:
                run = run_ref[rs, :]
                att = jnp.exp(ls_buf[u, rs, :] + run + sum_buf[u, rs, :bk])
                if masked:
                    att = jnp.where(causal(j, rs), att, 0.0)
                att_buf[u, rs, :] = att.astype(BF16)
                run_ref[rs, :] = run + sum_buf[u, rs, bk:]

        def output():
            acc_ref[...] += _dot(att_buf[u], v_ref[ks, :])

        return (scores, log_terms, suffix_sums, weights, output)

    def run_tiles(tiles, masked):
        per_tile = [stages(j, masked, u) for u, j in enumerate(tiles)]
        for stage in zip(*per_tile):
            for emit in stage:
                emit()

    run_tiles([(i + 1) * ratio - 1 - u for u in range(ratio)], True)
    n_off = i * ratio

    def body(step, carry):
        run_tiles([n_off - 1 - step * group - u for u in range(group)], False)
        return carry

    lax.fori_loop(0, n_off // group, body, 0)

    def leftover(step, carry):
        run_tiles([n_off % group - 1 - step], False)
        return carry

    lax.fori_loop(0, n_off % group, leftover, 0)
    o_ref[...] = jnp.where(head0, acc_ref[:bq, :], acc_ref[bq:, :])


def _sb_attn(q, k, v, seq_len):
    n, d = q.shape
    batch = n // seq_len
    pairs = d // LANES
    bq = min(SB_Q_BLOCK, seq_len)
    bk = SB_K_BLOCK
    nq = seq_len // bq
    in_flight = max(SB_GROUP, bq // bk)
    q_spec = pl.BlockSpec((bq, LANES), lambda p, i: ((p // pairs) * nq + i, p % pairs))
    kv_spec = pl.BlockSpec((seq_len, LANES), lambda p, i: (p // pairs, p % pairs))
    return pl.pallas_call(
        functools.partial(_sb_attn_kernel, bq=bq, bk=bk, group=SB_GROUP),
        grid=(batch * pairs, nq),
        in_specs=[q_spec, kv_spec, kv_spec],
        out_specs=q_spec,
        out_shape=jax.ShapeDtypeStruct((n, d), F32),
        scratch_shapes=[pltpu.VMEM((2 * bq, LANES), F32), pltpu.VMEM((2 * bq, LANES), F32),
                        pltpu.VMEM((in_flight, 2 * bq, bk), F32), pltpu.VMEM((in_flight, 2 * bq, 2 * bk), BF16),
                        pltpu.VMEM((in_flight, 2 * bq, 2 * bk), F32), pltpu.VMEM((in_flight, 2 * bq, bk), BF16)],
        compiler_params=_cparams(("parallel", "arbitrary")),
        name="sb_attn",
    )(q, k, v)


def _router_kernel(x_ref, w_ref, b_ref, idx_ref, gate_ref):
    logits = _dot_nt(w_ref[...], x_ref[...], HIGHEST) + b_ref[...]
    n_exp = logits.shape[0]
    e_idx = lax.broadcasted_iota(jnp.int32, logits.shape, 0)
    vals, idxs = [], []
    work = logits
    for _ in range(TOP_K):
        m = jnp.max(work, axis=0, keepdims=True)
        am = jnp.min(jnp.where(work == m, e_idx, n_exp), axis=0, keepdims=True)
        vals.append(m)
        idxs.append(am)
        work = jnp.where(e_idx == am, -jnp.inf, work)
    ex = [jnp.exp(vj - vals[0]) for vj in vals]
    inv = 1.0 / (ex[0] + ex[1] + ex[2] + ex[3])
    pad_i = jnp.zeros((8 - TOP_K, logits.shape[1]), jnp.int32)
    pad_f = jnp.zeros((8 - TOP_K, logits.shape[1]), F32)
    idx_ref[...] = jnp.concatenate(idxs + [pad_i], axis=0)
    gate_ref[...] = jnp.concatenate([e * inv for e in ex] + [pad_f], axis=0)


def _router(xf, w_router, b_router):
    n, d = xf.shape
    n_exp = w_router.shape[1]
    tm = min(ROW_TILE, n)
    out_spec = pl.BlockSpec((8, tm), lambda i: (0, i))
    idx, gate = pl.pallas_call(
        _router_kernel,
        grid=(n // tm,),
        in_specs=[pl.BlockSpec((tm, d), lambda i: (i, 0)), pl.BlockSpec((n_exp, d), lambda i: (0, 0)),
                  pl.BlockSpec((n_exp, 1), lambda i: (0, 0))],
        out_specs=[out_spec, out_spec],
        out_shape=[jax.ShapeDtypeStruct((8, n), jnp.int32), jax.ShapeDtypeStruct((8, n), F32)],
        compiler_params=_cparams(("parallel",)),
        name="moe_router",
    )(xf, w_router.T, b_router.reshape(n_exp, 1))
    return idx[:TOP_K].T, gate[:TOP_K].T


def _group_rows(top_idx, n_exp, rows):
    n = top_idx.shape[0]
    n_assign = n * TOP_K
    expert = top_idx.reshape(-1)
    order = jnp.argsort(expert, stable=True).astype(jnp.int32)
    counts = jnp.sum(expert[:, None] == jnp.arange(n_exp, dtype=jnp.int32)[None, :], axis=0, dtype=jnp.int32)
    padded = (counts + rows - 1) // rows * rows
    pad_end = jnp.cumsum(padded)
    pad_start = pad_end - padded
    start = jnp.cumsum(counts) - counts
    n_blocks = n_assign // rows + n_exp
    block_expert = jnp.minimum(
        jnp.searchsorted(pad_end, jnp.arange(n_blocks, dtype=jnp.int32) * rows, side="right"),
        n_exp - 1).astype(jnp.int32)
    pos = jnp.arange(n_blocks * rows, dtype=jnp.int32)
    blk = pos // rows
    e_p = block_expert[blk]
    off = pos - pad_start[e_p]
    valid = off < counts[e_p]
    assign = order[jnp.clip(start[e_p] + off, 0, n_assign - 1)]
    tok = assign // TOP_K
    slot = assign % TOP_K
    src = jnp.where(valid, tok, 0)
    dst = jnp.where(valid, slot * n + tok, 0)
    n_valid = jnp.sum(valid.reshape(n_blocks, rows), axis=1, dtype=jnp.int32)
    n_used = (pad_end[-1] // rows).astype(jnp.int32).reshape(1)
    return (src.reshape(n_blocks, 1, rows), dst.reshape(n_blocks, 1, rows), block_expert, n_valid, n_used)


def _swiglu_split_kernel(w_ref, even_ref, odd_ref, glu_ref, lin_ref):
    w = w_ref[0].astype(BF16)
    glu_ref[0] = _dot(w, even_ref[...]).astype(BF16)
    lin_ref[0] = _dot(w, odd_ref[...]).astype(BF16)


def _swiglu_split(w1):
    n_exp, d, f2 = w1.shape
    cols = min(SPLIT_COLS, f2)
    r = jnp.arange(cols, dtype=jnp.int32)[:, None]
    c = jnp.arange(cols // 2, dtype=jnp.int32)[None, :]
    even = (r == 2 * c).astype(BF16)
    odd = (r == 2 * c + 1).astype(BF16)
    sel = pl.BlockSpec((cols, cols // 2), lambda e, j: (0, 0))
    out_spec = pl.BlockSpec((1, d, cols // 2), lambda e, j: (e, 0, j))
    out = jax.ShapeDtypeStruct((n_exp, d, f2 // 2), BF16)
    return pl.pallas_call(
        _swiglu_split_kernel,
        grid=(n_exp, f2 // cols),
        in_specs=[pl.BlockSpec((1, d, cols), lambda e, j: (e, 0, j)), sel, sel],
        out_specs=[out_spec, out_spec],
        out_shape=[out, out],
        compiler_params=_cparams(("parallel", "parallel")),
        name="swiglu_split",
    )(w1, even, odd)


def _expert_kernel(bexp_ref, nvalid_ref, nused_ref, src_ref, src_next_ref, dst_ref, x_hbm, w1g_ref, w1l_ref,
                   b1g_ref, b1l_ref, w2_ref, b2_ref, y_hbm, xbuf, ybuf, gsem, ssem, *, rows):
    del bexp_ref
    b = pl.program_id(0)
    n_steps = pl.num_programs(0)
    n_used = nused_ref[0]
    slot = b % 2

    def gather(idx_ref, s):
        def body(i, carry):
            pltpu.make_async_copy(x_hbm.at[pl.ds(idx_ref[0, 0, i], 1), :], xbuf.at[s, pl.ds(i, 1), :],
                                  gsem.at[s]).start()
            return carry
        lax.fori_loop(0, rows, body, 0, unroll=DMA_UNROLL)

    def wait_gather(s):
        pltpu.make_async_copy(x_hbm.at[pl.ds(0, rows), :], xbuf.at[s], gsem.at[s]).wait()

    def scatter_row(i):
        pltpu.make_async_copy(ybuf.at[slot, pl.ds(i, 1), :], y_hbm.at[pl.ds(dst_ref[0, 0, i], 1), :],
                              ssem.at[slot]).start()

    def wait_scatter(s, blk):
        nv = nvalid_ref[blk]
        size = rows
        while size >= 1:
            @pl.when((nv & size) != 0)
            def _():
                pltpu.make_async_copy(ybuf.at[s, pl.ds(0, size), :], y_hbm.at[pl.ds(0, size), :],
                                      ssem.at[s]).wait()
            size //= 2

    @pl.when(jnp.logical_and(b == 0, n_used > 0))
    def _():
        gather(src_ref, 0)

    @pl.when(b + 1 < n_used)
    def _():
        gather(src_next_ref, 1 - slot)

    @pl.when(jnp.logical_and(b >= 2, b - 2 < n_used))
    def _():
        wait_scatter(slot, b - 2)

    @pl.when(b < n_used)
    def _():
        wait_gather(slot)
        xb = xbuf[slot].astype(BF16)
        glu = jnp.minimum(_dot(xb, w1g_ref[0]) + b1g_ref[0], SWIGLU_LIMIT)
        lin = jnp.clip(_dot(xb, w1l_ref[0]) + b1l_ref[0], -SWIGLU_LIMIT, SWIGLU_LIMIT)
        act = glu * _sigmoid(SWIGLU_ALPHA * glu) * (lin + 1.0)
        ybuf[slot] = _dot(act.astype(BF16), w2_ref[0]) + b2_ref[0]

        nv = nvalid_ref[b]

        def group(g, carry):
            for u in range(DMA_UNROLL):
                scatter_row(g * DMA_UNROLL + u)
            return carry
        lax.fori_loop(0, nv // DMA_UNROLL, group, 0)

        def single(i, carry):
            scatter_row(i)
            return carry
        lax.fori_loop(nv // DMA_UNROLL * DMA_UNROLL, nv, single, 0)

    @pl.when(b == n_steps - 1)
    def _():
        @pl.when(jnp.logical_and(b >= 1, b - 1 < n_used))
        def _():
            wait_scatter(1 - slot, b - 1)

        @pl.when(b < n_used)
        def _():
            wait_scatter(slot, b)


def _moe_experts(xf, src, dst, block_expert, n_valid, n_used, w1g, w1l, b1g, b1l, w2, b2):
    n, d = xf.shape
    n_blocks, _, rows = src.shape
    n_exp, _, f = w1g.shape
    idx_spec = lambda shift: pl.BlockSpec(
        (1, 1, rows), lambda b, be, nv, nu: (jnp.minimum(b + shift, n_blocks - 1), 0, 0), memory_space=pltpu.SMEM)
    per_expert = lambda shape: pl.BlockSpec((1,) + shape, lambda b, be, nv, nu: (be[b], 0, 0))
    grid_spec = pltpu.PrefetchScalarGridSpec(
        num_scalar_prefetch=3,
        grid=(n_blocks,),
        in_specs=[idx_spec(0), idx_spec(1), idx_spec(0), pl.BlockSpec(memory_space=pl.ANY),
                  per_expert((d, f)), per_expert((d, f)), per_expert((1, f)), per_expert((1, f)),
                  per_expert((f, d)), per_expert((1, d))],
        out_specs=pl.BlockSpec(memory_space=pl.ANY),
        scratch_shapes=[pltpu.VMEM((2, rows, d), F32), pltpu.VMEM((2, rows, d), F32),
                        pltpu.SemaphoreType.DMA((2,)), pltpu.SemaphoreType.DMA((2,))],
    )
    return pl.pallas_call(
        functools.partial(_expert_kernel, rows=rows),
        grid_spec=grid_spec,
        out_shape=jax.ShapeDtypeStruct((TOP_K * n, d), F32),
        compiler_params=_cparams(("arbitrary",)),
        name="moe_experts",
    )(block_expert, n_valid, n_used, src, src, dst, xf, w1g, w1l, b1g, b1l, w2, b2)


def _combine_kernel(x_ref, gate_ref, y0_ref, y1_ref, y2_ref, y3_ref, lg_ref, lb_ref, out_ref):
    gate = gate_ref[...]
    f = (gate[:, 0:1] * y0_ref[...] + gate[:, 1:2] * y1_ref[...]
         + gate[:, 2:3] * y2_ref[...] + gate[:, 3:4] * y3_ref[...])
    out_ref[...] = _layer_norm(ALPHA * x_ref[...] + f, lg_ref[...], lb_ref[...])


def _moe_combine(xf, gate, y, ln_g, ln_b):
    n, d = xf.shape
    tm = min(ROW_TILE, n)
    tiles = n // tm
    row_tile = pl.BlockSpec((tm, d), lambda i: (i, 0))
    slot_tile = lambda j: pl.BlockSpec((tm, d), lambda i: (j * tiles + i, 0))
    vec = pl.BlockSpec((1, d), lambda i: (0, 0))
    return pl.pallas_call(
        _combine_kernel,
        grid=(tiles,),
        in_specs=[row_tile, pl.BlockSpec((tm, TOP_K), lambda i: (i, 0))] + [slot_tile(j) for j in range(TOP_K)]
                 + [vec, vec],
        out_specs=row_tile,
        out_shape=jax.ShapeDtypeStruct((n, d), F32),
        compiler_params=_cparams(("parallel",)),
        name="moe_combine",
    )(xf, gate, y, y, y, y, ln_g.reshape(1, d), ln_b.reshape(1, d))


def _moe(xf, w_router, b_router, w1g, w1l, b1, w2, b2, ln_g, ln_b):
    n, d = xf.shape
    n_exp = w_router.shape[1]
    rows = min(EXPERT_ROWS, n)
    top_idx, gate = _router(xf, w_router, b_router)
    src, dst, block_expert, n_valid, n_used = _group_rows(top_idx, n_exp, rows)
    y = _moe_experts(xf, src, dst, block_expert, n_valid, n_used, w1g, w1l, b1[:, None, 0::2],
                     b1[:, None, 1::2], w2.astype(BF16), b2[:, None, :])
    return _moe_combine(xf, gate, y, ln_g, ln_b)


def kernel(x, rw_mix, rw_w_rkv, rw_w0, rw_w1, rw_w2, rw_a0, rw_a1, rw_a2, rw_g1, rw_g2, rw_k_k, rw_k_a, rw_r_k,
           rw_lnx_g, rw_lnx_b, rw_w_o, sb_w_qkv, sb_w_o, moe_w_router, moe_b_router, moe_w1, moe_b1, moe_w2,
           moe_b2, ln_g, ln_b):
    batch, seq_len, d = x.shape
    xf = x.reshape(batch * seq_len, d)
    n_layers, n_exp = moe_w1.shape[:2]
    w1g, w1l = _swiglu_split(moe_w1.reshape((n_layers * n_exp,) + moe_w1.shape[2:]))
    w1g = w1g.reshape((n_layers, n_exp) + w1g.shape[1:])
    w1l = w1l.reshape((n_layers, n_exp) + w1l.shape[1:])
    for i in range(DEPTH):
        j = i // 2
        if i % 2 == 0:
            r, k, v, ld, a, g = _rwkv_proj(xf, seq_len, rw_mix[j], rw_w_rkv[j], rw_w0[j], rw_w1[j], rw_w2[j],
                                           rw_a0[j], rw_a1[j], rw_a2[j], rw_g1[j], rw_g2[j])
            o = _rwkv_scan(r, k, v, ld, a, seq_len, rw_k_k[j], rw_k_a[j], rw_r_k[j], rw_lnx_g[j], rw_lnx_b[j])
            xf = _proj_ln(xf, o, g, rw_w_o[j], ln_g[i, 0], ln_b[i, 0])
        else:
            q, k, v = _sb_qkv(xf, sb_w_qkv[j])
            o = _sb_attn(q, k, v, seq_len)
            xf = _proj_ln(xf, o, None, sb_w_o[j], ln_g[i, 0], ln_b[i, 0])
        xf = _moe(xf, moe_w_router[i], moe_b_router[i], w1g[i], w1l[i], moe_b1[i], moe_w2[i], moe_b2[i],
                  ln_g[i, 1], ln_b[i, 1])
    return xf.reshape(batch, seq_len, d)
```

```python
import functools

import jax
import jax.numpy as jnp
from jax import lax
from jax.experimental import pallas as pl
from jax.experimental.pallas import tpu as pltpu

F32 = jnp.float32
BF16 = jnp.bfloat16
HIGHEST = lax.Precision.HIGHEST

DEPTH = 2
ALPHA = (2 * DEPTH) ** 0.25
LN_EPS = 1e-5
GN_EPS = 64e-5
HEAD_DIM = 64
LANES = 128
TOP_K = 4
SWIGLU_LIMIT = 7.0
SWIGLU_ALPHA = 1.702

ROW_TILE = 256
SCAN_CHUNK = 64
SCAN_ROWS = 512
SCAN_UNROLL = 8
SB_Q_BLOCK = 256
SB_K_BLOCK = LANES
SB_GROUP = 2
SB_EXP_ZERO = -104.0
SB_ROW_CHUNK = 64
EXPERT_ROWS = 256
DMA_UNROLL = 8
SPLIT_COLS = 512
VMEM_LIMIT = 56 * 1024 * 1024


def _cparams(semantics):
    return pltpu.CompilerParams(dimension_semantics=semantics, vmem_limit_bytes=VMEM_LIMIT)


def _dot(a, b, precision=None):
    return jnp.dot(a, b, preferred_element_type=F32, precision=precision)


def _dot_nt(a, b, precision=None):
    return lax.dot_general(a, b, (((1,), (1,)), ((), ())), preferred_element_type=F32, precision=precision)


def _dot_tn(a, b, precision=None):
    return lax.dot_general(a, b, (((0,), (0,)), ((), ())), preferred_element_type=F32, precision=precision)


def _split2(a):
    hi = a.astype(BF16)
    return hi, (a - hi.astype(F32)).astype(BF16)


def _split3(a):
    hi = a.astype(BF16)
    rest = a - hi.astype(F32)
    mid = rest.astype(BF16)
    return hi, mid, (rest - mid.astype(F32)).astype(BF16)


def _mm3(a, b, dot=_dot):
    return dot(a[0], b[0]) + (dot(a[0], b[1]) + dot(a[1], b[0]))


def _softplus(u):
    return jnp.maximum(u, 0.0) + jnp.log(1.0 + jnp.exp(-jnp.abs(u)))


def _sigmoid(u):
    return 1.0 / (1.0 + jnp.exp(-u))


def _layer_norm(y, g, b):
    mu = jnp.mean(y, axis=-1, keepdims=True)
    d = y - mu
    var = jnp.mean(d * d, axis=-1, keepdims=True)
    return d * lax.rsqrt(var + LN_EPS) * g + b


def _head_block_ones():
    r = lax.broadcasted_iota(jnp.int32, (LANES, LANES), 0) // HEAD_DIM
    c = lax.broadcasted_iota(jnp.int32, (LANES, LANES), 1) // HEAD_DIM
    return (r == c).astype(BF16)


def _head_sums(x, head_ones):
    m = x.shape[0]
    s = _dot(jnp.concatenate(_split3(x), axis=0), head_ones)
    return s[:m] + s[m:2 * m] + s[2 * m:]


def _rwkv_proj_kernel(x_ref, xp_ref, mix_ref, wrkv_ref, w0_ref, w1_ref, w2_ref, a0_ref, a1_ref, a2_ref,
                      g1_ref, g2_ref, r_ref, k_ref, v_ref, ld_ref, a_ref, g_ref, *, tiles_per_seq):
    i = pl.program_id(0)
    x = x_ref[...]
    prev_last = jnp.where(i % tiles_per_seq == 0, 0.0, xp_ref[7:8, :])
    row = lax.broadcasted_iota(jnp.int32, x.shape, 0)
    x_prev = jnp.where(row == 0, prev_last, pltpu.roll(x, 1, 0))
    xx = x_prev - x

    def mixed(j):
        return (x + xx * mix_ref[j:j + 1, :]).astype(BF16)

    r_ref[...] = _dot(mixed(0), wrkv_ref[0])
    w = w0_ref[...] + _dot(jnp.tanh(_dot(mixed(1), w1_ref[...])).astype(BF16), w2_ref[...])
    ld_ref[...] = -jnp.exp(-_softplus(-w) - 0.5)
    k_ref[...] = _dot(mixed(2), wrkv_ref[1])
    v_ref[...] = _dot(mixed(3), wrkv_ref[2])
    a_ref[...] = _sigmoid(a0_ref[...] + _dot(_dot(mixed(4), a1_ref[...]).astype(BF16), a2_ref[...]))
    g_ref[...] = _dot(_sigmoid(_dot(mixed(5), g1_ref[...])).astype(BF16), g2_ref[...])


def _rwkv_proj(xf, seq_len, mix, w_rkv, w0, w1, w2, a0, a1, a2, g1, g2):
    n, d = xf.shape
    tm = min(ROW_TILE, seq_len)
    full = lambda shape: pl.BlockSpec(shape, lambda i: (0,) * len(shape))
    row_tile = pl.BlockSpec((tm, d), lambda i: (i, 0))
    prev_rows = pl.BlockSpec((8, d), lambda i: (jnp.maximum(i * (tm // 8) - 1, 0), 0))
    mix8 = jnp.concatenate([mix, jnp.zeros((2, d), F32)], axis=0)
    out = jax.ShapeDtypeStruct((n, d), F32)
    return pl.pallas_call(
        functools.partial(_rwkv_proj_kernel, tiles_per_seq=seq_len // tm),
        grid=(n // tm,),
        in_specs=[row_tile, prev_rows, full((8, d)), full(w_rkv.shape), full((1, d)), full(w1.shape),
                  full(w2.shape), full((1, d)), full(a1.shape), full(a2.shape), full(g1.shape), full(g2.shape)],
        out_specs=[row_tile] * 6,
        out_shape=[out] * 6,
        compiler_params=_cparams(("parallel",)),
        name="rwkv_proj",
    )(xf, xf, mix8, w_rkv.astype(BF16), w0.reshape(1, d), w1.astype(BF16), w2.astype(BF16),
      a0.reshape(1, d), a1.astype(BF16), a2.astype(BF16), g1.astype(BF16), g2.astype(BF16))


def _rwkv_scan_kernel(r_ref, k_ref, v_ref, ld_ref, a_ref, kk_ref, ka_ref, rk_ref, lg_ref, lb_ref,
                      o_ref, s_ref, rhat_ref, o0_ref, p_ref, q_ref, dcol_ref, *, chunk, n_chunks, unroll):
    L = chunk

    @pl.when(pl.program_id(1) == 0)
    def _():
        s_ref[...] = jnp.zeros_like(s_ref)

    head0 = lax.broadcasted_iota(jnp.int32, (1, LANES), 1) < HEAD_DIM
    row = lax.broadcasted_iota(jnp.int32, (2 * L, 2 * L), 0)
    col = lax.broadcasted_iota(jnp.int32, (2 * L, 2 * L), 1)
    same_head = (row // L) == (col // L)
    lower_strict = jnp.logical_and(same_head, col < row)
    lower_incl = jnp.logical_and(same_head, col <= row)
    eye = (row == col).astype(F32)
    cum_rows = (lax.broadcasted_iota(jnp.int32, (L, L), 1)
                <= lax.broadcasted_iota(jnp.int32, (L, L), 0)).astype(BF16)
    head_ones = _head_block_ones()
    ones_ln = jnp.ones((L, LANES), BF16)

    def stack(t):
        return jnp.concatenate([jnp.where(head0, t, 0.0), jnp.where(head0, 0.0, t)], axis=0)

    def modified_key(k, a):
        return k * (1.0 + (a - 1.0) * ka_ref[...])

    def prepare(ci):
        sl = pl.ds(pl.multiple_of(ci * L, L), L)
        r, k, v, ld, a = r_ref[sl, :], k_ref[sl, :], v_ref[sl, :], ld_ref[sl, :], a_ref[sl, :]
        kk = k * kk_ref[...]
        kk_sq = _head_sums(kk * kk, head_ones)
        ld3 = jnp.concatenate(_split3(ld), axis=1)
        c3 = _dot(cum_rows, ld3)
        tot3 = _dot_tn(ld3, ones_ln)
        yield
        kk = kk / jnp.maximum(jnp.sqrt(kk_sq), 1e-12)
        k2 = modified_key(k, a)
        a_vec, b_vec = -kk, kk * a
        c = c3[:, :LANES] + c3[:, LANES:2 * LANES] + c3[:, 2 * LANES:]
        dcol_ref[ci] = jnp.exp(tot3[:LANES] + tot3[LANES:2 * LANES] + tot3[2 * LANES:])
        to_end = jnp.exp(c[L - 1:L, :] - c)
        inv = jnp.exp(-c)
        a_st = stack(a_vec * jnp.exp(c - ld))
        r_st = stack(r * jnp.exp(c))
        v_sp = _split2(stack(v))
        bh_sp = _split2(stack(b_vec * to_end))
        kh_sp = _split2(stack(k2 * to_end))
        ar = _split2(jnp.concatenate([a_st, r_st], axis=0))
        bk = _split2(jnp.concatenate([stack(b_vec * inv), stack(k2 * inv)], axis=0))
        prod = _mm3(ar, bk, _dot_nt)
        kv = _mm3(kh_sp, v_sp, _dot_tn)
        yield
        m_ab = jnp.where(lower_strict, prod[:2 * L, :2 * L], 0.0)
        m_ak = jnp.where(lower_strict, prod[:2 * L, 2 * L:], 0.0)
        a_rb = jnp.where(lower_incl, prod[2 * L:, :2 * L], 0.0)
        a_rk = jnp.where(lower_incl, prod[2 * L:, 2 * L:], 0.0)
        w0 = _mm3(_split2(m_ak), v_sp)
        rkv = _mm3(_split2(a_rk), v_sp)
        t_inv = eye
        w = 1
        while w < L:
            below = jnp.logical_and((row // (2 * w)) == (col // (2 * w)),
                                    jnp.logical_and((row % (2 * w)) >= w, (col % (2 * w)) < w))
            t_sp = _split2(t_inv)
            tm = _mm3(t_sp, _split2(jnp.where(below, m_ab, 0.0)))
            yield
            tmt = _mm3(_split2(tm), t_sp)
            yield
            t_inv = t_inv + tmt
            w *= 2
        au = _mm3(_split2(t_inv), _split2(jnp.concatenate([a_st, w0], axis=1)))
        yield
        au_sp = _split2(au)
        ro = _mm3(_split2(a_rb), au_sp)
        pq = _mm3(bh_sp, au_sp, _dot_tn)
        yield
        rhat_ref[ci] = r_st + ro[:, :LANES]
        o0_ref[ci] = ro[:, LANES:] + rkv
        p_ref[ci] = pq[:, :LANES]
        q_ref[ci] = pq[:, LANES:] + kv

    def prepare_group(gi, carry):
        chunks = [prepare(gi * unroll + u) for u in range(unroll)]
        while chunks:
            for g in chunks:
                if next(g, StopIteration) is StopIteration:
                    chunks = []
        return carry

    lax.fori_loop(0, n_chunks // unroll, prepare_group, 0)

    s = s_ref[...]
    for ci in range(n_chunks):
        s_sp = _split2(s)
        o_st = _mm3(_split2(rhat_ref[ci]), s_sp) + o0_ref[ci]
        o_ref[ci * L:(ci + 1) * L, :] = o_st[:L] + o_st[L:]
        s = dcol_ref[ci] * s + _mm3(_split2(p_ref[ci]), s_sp) + q_ref[ci]
    s_ref[...] = s

    o = o_ref[...]
    inv_n = 1.0 / HEAD_DIM
    mu = _head_sums(o, head_ones) * inv_n
    d = o - mu
    var = _head_sums(d * d, head_ones) * inv_n
    r, v = r_ref[...], v_ref[...]
    k2 = modified_key(k_ref[...], a_ref[...])
    bonus = _head_sums(r * k2 * rk_ref[...], head_ones) * v
    o_ref[...] = d * lax.rsqrt(var + GN_EPS) * lg_ref[...] + lb_ref[...] + bonus


def _rwkv_scan(r, k, v, ld, a, seq_len, k_k, k_a, r_k, lnx_g, lnx_b):
    n, d = r.shape
    batch = n // seq_len
    pairs = d // LANES
    rows = min(SCAN_ROWS, seq_len)
    steps = seq_len // rows
    n_chunks = rows // SCAN_CHUNK
    seq = pl.BlockSpec((rows, LANES), lambda p, c: ((p // pairs) * steps + c, p % pairs))
    par = pl.BlockSpec((1, LANES), lambda p, c: (0, p % pairs))
    per_chunk = pltpu.VMEM((n_chunks, 2 * SCAN_CHUNK, LANES), F32)
    return pl.pallas_call(
        functools.partial(_rwkv_scan_kernel, chunk=SCAN_CHUNK, n_chunks=n_chunks,
                          unroll=min(SCAN_UNROLL, n_chunks)),
        grid=(batch * pairs, steps),
        in_specs=[seq] * 5 + [par] * 5,
        out_specs=seq,
        out_shape=jax.ShapeDtypeStruct((n, d), F32),
        scratch_shapes=[pltpu.VMEM((LANES, LANES), F32)] + [per_chunk] * 5,
        compiler_params=_cparams(("parallel", "arbitrary")),
        name="rwkv_scan",
    )(r, k, v, ld, a, k_k.reshape(1, d), k_a.reshape(1, d), r_k.reshape(1, d),
      lnx_g.reshape(1, d), lnx_b.reshape(1, d))


def _proj_ln_kernel(*refs, gated):
    if gated:
        x_ref, o_ref, g_ref, w_ref, lg_ref, lb_ref, out_ref = refs
        h = o_ref[...] * g_ref[...]
    else:
        x_ref, o_ref, w_ref, lg_ref, lb_ref, out_ref = refs
        h = o_ref[...]
    y = ALPHA * x_ref[...] + _dot(h.astype(BF16), w_ref[...])
    out_ref[...] = _layer_norm(y, lg_ref[...], lb_ref[...])


def _proj_ln(xf, o, g, w_o, ln_g, ln_b):
    n, d = xf.shape
    tm = min(ROW_TILE, n)
    row_tile = pl.BlockSpec((tm, d), lambda i: (i, 0))
    full = lambda shape: pl.BlockSpec(shape, lambda i: (0,) * len(shape))
    gated = g is not None
    acts = (xf, o, g) if gated else (xf, o)
    return pl.pallas_call(
        functools.partial(_proj_ln_kernel, gated=gated),
        grid=(n // tm,),
        in_specs=[row_tile] * len(acts) + [full((d, d)), full((1, d)), full((1, d))],
        out_specs=row_tile,
        out_shape=jax.ShapeDtypeStruct((n, d), F32),
        compiler_params=_cparams(("parallel",)),
        name="proj_ln",
    )(*acts, w_o.astype(BF16), ln_g.reshape(1, d), ln_b.reshape(1, d))


def _sb_qkv_kernel(x_ref, w_ref, q_ref, k_ref, v_ref):
    d = x_ref.shape[1]
    qkv = _dot(x_ref[...].astype(BF16), w_ref[...])
    q_ref[...] = (qkv[:, :d] * (HEAD_DIM ** -0.5)).astype(BF16)
    k_ref[...] = qkv[:, d:2 * d].astype(BF16)
    v_ref[...] = qkv[:, 2 * d:].astype(BF16)


def _sb_qkv(xf, w_qkv):
    n, d = xf.shape
    tm = min(ROW_TILE, n)
    row_tile = pl.BlockSpec((tm, d), lambda i: (i, 0))
    out = jax.ShapeDtypeStruct((n, d), BF16)
    return pl.pallas_call(
        _sb_qkv_kernel,
        grid=(n // tm,),
        in_specs=[row_tile, pl.BlockSpec((d, 3 * d), lambda i: (0, 0))],
        out_specs=[row_tile] * 3,
        out_shape=[out] * 3,
        compiler_params=_cparams(("parallel",)),
        name="sb_qkv",
    )(xf, w_qkv.astype(BF16))


def _sb_attn_kernel(q_ref, k_ref, v_ref, o_ref, acc_ref, run_ref, ls_buf, lk_buf, sum_buf, att_buf,
                    *, bq, bk, group):
    i = pl.program_id(1)
    ratio = bq // bk
    m = 2 * bq
    head0 = lax.broadcasted_iota(jnp.int32, (1, LANES), 1) < HEAD_DIM
    q = q_ref[...]
    zero = jnp.zeros_like(q)
    q2 = jnp.concatenate([jnp.where(head0, q, zero), jnp.where(head0, zero, q)], axis=0)
    kr = lax.broadcasted_iota(jnp.int32, (2 * bk, 2 * bk), 0) % bk
    kc = lax.broadcasted_iota(jnp.int32, (2 * bk, 2 * bk), 1)
    suffix = jnp.logical_or(kr > kc, kc >= bk).astype(BF16)
    acc_ref[...] = jnp.zeros_like(acc_ref)
    run_ref[...] = jnp.zeros_like(run_ref)
    row_chunks = [slice(c, c + SB_ROW_CHUNK) for c in range(0, m, SB_ROW_CHUNK)]

    def causal(j, rs):
        q_pos = i * bq + (lax.broadcasted_iota(jnp.int32, (SB_ROW_CHUNK, bk), 0) + rs.start) % bq
        return (j * bk + lax.broadcasted_iota(jnp.int32, (SB_ROW_CHUNK, bk), 1)) < q_pos

    def stages(j, masked, u):
        ks = pl.ds(pl.multiple_of(j * bk, bk), bk)

        def scores():
            ls_buf[u] = _dot_nt(q2, k_ref[ks, :])

        def log_terms():
            for rs in row_chunks:
                z = ls_buf[u, rs, :]
                log_keep = -_softplus(z)
                if masked:
                    log_keep = jnp.where(causal(j, rs), log_keep, 0.0)
                hi, lo = _split2(log_keep)
                lk_buf[u, rs, :bk] = hi
                lk_buf[u, rs, bk:] = lo
                ls_buf[u, rs, :] = z + log_keep

        def suffix_sums():
            sum_buf[u] = _dot(lk_buf[u], suffix)

        def weights():
            for rs in row_chunks:
                run = run_ref[rs, :]
                att = jnp.exp(ls_buf[u, rs, :] + run + sum_buf[u, rs, :bk])
                if masked:
                    att = jnp.where(causal(j, rs), att, 0.0)
                att_buf[u, rs, :] = att.astype(BF16)
                run_ref[rs, :] = run + sum_buf[u, rs, bk:]

        def output():
            acc_ref[...] += _dot(att_buf[u], v_ref[ks, :])

        return (scores, log_terms, suffix_sums, weights, output)

    def run_tiles(tiles, masked):
        per_tile = [stages(j, masked, u) for u, j in enumerate(tiles)]
        for stage in zip(*per_tile):
            for emit in stage:
                emit()

    run_tiles([(i + 1) * ratio - 1 - u for u in range(ratio)], True)
    n_off = i * ratio

    def live(step):
        return jnp.logical_and(step < n_off // group, jnp.max(run_ref[...]) > SB_EXP_ZERO)

    def body(step):
        run_tiles([n_off - 1 - step * group - u for u in range(group)], False)
        return step + 1

    lax.while_loop(live, body, 0)
    o_ref[...] = jnp.where(head0, acc_ref[:bq, :], acc_ref[bq:, :])


def _sb_attn(q, k, v, seq_len):
    n, d = q.shape
    batch = n // seq_len
    pairs = d // LANES
    bq = min(SB_Q_BLOCK, seq_len)
    bk = SB_K_BLOCK
    nq = seq_len // bq
    ratio = bq // bk
    group = min(SB_GROUP, ratio)
    assert ratio % group == 0
    in_flight = ratio
    q_spec = pl.BlockSpec((bq, LANES), lambda p, i: ((p // pairs) * nq + i, p % pairs))
    kv_spec = pl.BlockSpec((seq_len, LANES), lambda p, i: (p // pairs, p % pairs))
    return pl.pallas_call(
        functools.partial(_sb_attn_kernel, bq=bq, bk=bk, group=group),
        grid=(batch * pairs, nq),
        in_specs=[q_spec, kv_spec, kv_spec],
        out_specs=q_spec,
        out_shape=jax.ShapeDtypeStruct((n, d), F32),
        scratch_shapes=[pltpu.VMEM((2 * bq, LANES), F32), pltpu.VMEM((2 * bq, LANES), F32),
                        pltpu.VMEM((in_flight, 2 * bq, bk), F32), pltpu.VMEM((in_flight, 2 * bq, 2 * bk), BF16),
                        pltpu.VMEM((in_flight, 2 * bq, 2 * bk), F32), pltpu.VMEM((in_flight, 2 * bq, bk), BF16)],
        compiler_params=_cparams(("parallel", "arbitrary")),
        name="sb_attn",
    )(q, k, v)


def _router_kernel(x_ref, w_ref, b_ref, idx_ref, gate_ref):
    logits = _dot_nt(w_ref[...], x_ref[...], HIGHEST) + b_ref[...]
    n_exp = logits.shape[0]
    e_idx = lax.broadcasted_iota(jnp.int32, logits.shape, 0)
    vals, idxs = [], []
    work = logits
    for _ in range(TOP_K):
        m = jnp.max(work, axis=0, keepdims=True)
        am = jnp.min(jnp.where(work == m, e_idx, n_exp), axis=0, keepdims=True)
        vals.append(m)
        idxs.append(am)
        work = jnp.where(e_idx == am, -jnp.inf, work)
    ex = [jnp.exp(vj - vals[0]) for vj in vals]
    inv = 1.0 / (ex[0] + ex[1] + ex[2] + ex[3])
    pad_i = jnp.zeros((8 - TOP_K, logits.shape[1]), jnp.int32)
    pad_f = jnp.zeros((8 - TOP_K, logits.shape[1]), F32)
    idx_ref[...] = jnp.concatenate(idxs + [pad_i], axis=0)
    gate_ref[...] = jnp.concatenate([e * inv for e in ex] + [pad_f], axis=0)


def _router(xf, w_router, b_router):
    n, d = xf.shape
    n_exp = w_router.shape[1]
    tm = min(ROW_TILE, n)
    out_spec = pl.BlockSpec((8, tm), lambda i: (0, i))
    idx, gate = pl.pallas_call(
        _router_kernel,
        grid=(n // tm,),
        in_specs=[pl.BlockSpec((tm, d), lambda i: (i, 0)), pl.BlockSpec((n_exp, d), lambda i: (0, 0)),
                  pl.BlockSpec((n_exp, 1), lambda i: (0, 0))],
        out_specs=[out_spec, out_spec],
        out_shape=[jax.ShapeDtypeStruct((8, n), jnp.int32), jax.ShapeDtypeStruct((8, n), F32)],
        compiler_params=_cparams(("parallel",)),
        name="moe_router",
    )(xf, w_router.T, b_router.reshape(n_exp, 1))
    return idx[:TOP_K].T, gate[:TOP_K].T


def _group_rows(top_idx, n_exp, rows):
    n = top_idx.shape[0]
    n_assign = n * TOP_K
    expert = top_idx.reshape(-1)
    order = jnp.argsort(expert, stable=True).astype(jnp.int32)
    counts = jnp.sum(expert[:, None] == jnp.arange(n_exp, dtype=jnp.int32)[None, :], axis=0, dtype=jnp.int32)
    padded = (counts + rows - 1) // rows * rows
    pad_end = jnp.cumsum(padded)
    pad_start = pad_end - padded
    n_blocks = n_assign // rows + n_exp
    block_pos = jnp.arange(n_blocks, dtype=jnp.int32)[:, None] * rows
    block_expert = jnp.minimum(jnp.sum(block_pos >= pad_end[None, :], axis=1, dtype=jnp.int32), n_exp - 1)
    pos = jnp.arange(n_blocks * rows, dtype=jnp.int32)
    blk = pos // rows
    past = pos[:, None] >= pad_end[None, :]
    n_past = jnp.sum(past, axis=1, dtype=jnp.int32)
    gap = jnp.sum(jnp.where(past, (padded - counts)[None, :], 0), axis=1, dtype=jnp.int32)
    n_real_end = jnp.sum(pos[:, None] >= (pad_start + counts)[None, :], axis=1, dtype=jnp.int32)
    valid = jnp.logical_and(n_real_end == n_past, pos < pad_end[-1])
    assign = order[jnp.clip(pos - gap, 0, n_assign - 1)]
    tok = assign // TOP_K
    slot = assign % TOP_K
    src = jnp.where(valid, tok, 0)
    dst = jnp.where(valid, slot * n + tok, 0)
    n_valid = jnp.sum(valid.reshape(n_blocks, rows), axis=1, dtype=jnp.int32)
    n_used = (pad_end[-1] // rows).astype(jnp.int32).reshape(1)
    return (src.reshape(n_blocks, 1, rows), dst.reshape(n_blocks, 1, rows), block_expert, n_valid, n_used)


def _swiglu_split_kernel(w_ref, even_ref, odd_ref, glu_ref, lin_ref):
    w = w_ref[0].astype(BF16)
    glu_ref[0] = _dot(w, even_ref[...]).astype(BF16)
    lin_ref[0] = _dot(w, odd_ref[...]).astype(BF16)


def _swiglu_split(w1):
    n_exp, d, f2 = w1.shape
    cols = min(SPLIT_COLS, f2)
    r = jnp.arange(cols, dtype=jnp.int32)[:, None]
    c = jnp.arange(cols // 2, dtype=jnp.int32)[None, :]
    even = (r == 2 * c).astype(BF16)
    odd = (r == 2 * c + 1).astype(BF16)
    sel = pl.BlockSpec((cols, cols // 2), lambda e, j: (0, 0))
    out_spec = pl.BlockSpec((1, d, cols // 2), lambda e, j: (e, 0, j))
    out = jax.ShapeDtypeStruct((n_exp, d, f2 // 2), BF16)
    return pl.pallas_call(
        _swiglu_split_kernel,
        grid=(n_exp, f2 // cols),
        in_specs=[pl.BlockSpec((1, d, cols), lambda e, j: (e, 0, j)), sel, sel],
        out_specs=[out_spec, out_spec],
        out_shape=[out, out],
        compiler_params=_cparams(("parallel", "parallel")),
        name="swiglu_split",
    )(w1, even, odd)


def _expert_kernel(bexp_ref, nvalid_ref, nused_ref, src_ref, src_next_ref, dst_ref, x_hbm, w1g_ref, w1l_ref,
                   b1g_ref, b1l_ref, w2_ref, b2_ref, y_hbm, xbuf, ybuf, gsem, ssem, *, rows):
    del bexp_ref
    b = pl.program_id(0)
    n_steps = pl.num_programs(0)
    n_used = nused_ref[0]
    slot = b % 2

    def gather(idx_ref, s):
        def body(i, carry):
            pltpu.make_async_copy(x_hbm.at[pl.ds(idx_ref[0, 0, i], 1), :], xbuf.at[s, pl.ds(i, 1), :],
                                  gsem.at[s]).start()
            return carry
        lax.fori_loop(0, rows, body, 0, unroll=DMA_UNROLL)

    def wait_gather(s):
        pltpu.make_async_copy(x_hbm.at[pl.ds(0, rows), :], xbuf.at[s], gsem.at[s]).wait()

    def scatter_row(i):
        pltpu.make_async_copy(ybuf.at[slot, pl.ds(i, 1), :], y_hbm.at[pl.ds(dst_ref[0, 0, i], 1), :],
                              ssem.at[slot]).start()

    def wait_scatter(s, blk):
        nv = nvalid_ref[blk]
        size = rows
        while size >= 1:
            @pl.when((nv & size) != 0)
            def _():
                pltpu.make_async_copy(ybuf.at[s, pl.ds(0, size), :], y_hbm.at[pl.ds(0, size), :],
                                      ssem.at[s]).wait()
            size //= 2

    @pl.when(jnp.logical_and(b == 0, n_used > 0))
    def _():
        gather(src_ref, 0)

    @pl.when(b + 1 < n_used)
    def _():
        gather(src_next_ref, 1 - slot)

    @pl.when(jnp.logical_and(b >= 2, b - 2 < n_used))
    def _():
        wait_scatter(slot, b - 2)

    @pl.when(b < n_used)
    def _():
        wait_gather(slot)
        xb = xbuf[slot].astype(BF16)
        glu = jnp.minimum(_dot(xb, w1g_ref[0]) + b1g_ref[0], SWIGLU_LIMIT)
        lin = jnp.clip(_dot(xb, w1l_ref[0]) + b1l_ref[0], -SWIGLU_LIMIT, SWIGLU_LIMIT)
        act = glu * _sigmoid(SWIGLU_ALPHA * glu) * (lin + 1.0)
        ybuf[slot] = _dot(act.astype(BF16), w2_ref[0]) + b2_ref[0]

        nv = nvalid_ref[b]

        def group(g, carry):
            for u in range(DMA_UNROLL):
                scatter_row(g * DMA_UNROLL + u)
            return carry
        lax.fori_loop(0, nv // DMA_UNROLL, group, 0)

        def single(i, carry):
            scatter_row(i)
            return carry
        lax.fori_loop(nv // DMA_UNROLL * DMA_UNROLL, nv, single, 0)

    @pl.when(b == n_steps - 1)
    def _():
        @pl.when(jnp.logical_and(b >= 1, b - 1 < n_used))
        def _():
            wait_scatter(1 - slot, b - 1)

        @pl.when(b < n_used)
        def _():
            wait_scatter(slot, b)


def _moe_experts(xf, src, dst, block_expert, n_valid, n_used, w1g, w1l, b1g, b1l, w2, b2):
    n, d = xf.shape
    n_blocks, _, rows = src.shape
    n_exp, _, f = w1g.shape
    idx_spec = lambda shift: pl.BlockSpec(
        (1, 1, rows), lambda b, be, nv, nu: (jnp.minimum(b + shift, n_blocks - 1), 0, 0), memory_space=pltpu.SMEM)
    per_expert = lambda shape: pl.BlockSpec((1,) + shape, lambda b, be, nv, nu: (be[b], 0, 0))
    grid_spec = pltpu.PrefetchScalarGridSpec(
        num_scalar_prefetch=3,
        grid=(n_blocks,),
        in_specs=[idx_spec(0), idx_spec(1), idx_spec(0), pl.BlockSpec(memory_space=pl.ANY),
                  per_expert((d, f)), per_expert((d, f)), per_expert((1, f)), per_expert((1, f)),
                  per_expert((f, d)), per_expert((1, d))],
        out_specs=pl.BlockSpec(memory_space=pl.ANY),
        scratch_shapes=[pltpu.VMEM((2, rows, d), F32), pltpu.VMEM((2, rows, d), F32),
                        pltpu.SemaphoreType.DMA((2,)), pltpu.SemaphoreType.DMA((2,))],
    )
    return pl.pallas_call(
        functools.partial(_expert_kernel, rows=rows),
        grid_spec=grid_spec,
        out_shape=jax.ShapeDtypeStruct((TOP_K * n, d), F32),
        compiler_params=_cparams(("arbitrary",)),
        name="moe_experts",
    )(block_expert, n_valid, n_used, src, src, dst, xf, w1g, w1l, b1g, b1l, w2, b2)


def _combine_kernel(x_ref, gate_ref, y0_ref, y1_ref, y2_ref, y3_ref, lg_ref, lb_ref, out_ref):
    gate = gate_ref[...]
    f = (gate[:, 0:1] * y0_ref[...] + gate[:, 1:2] * y1_ref[...]
         + gate[:, 2:3] * y2_ref[...] + gate[:, 3:4] * y3_ref[...])
    out_ref[...] = _layer_norm(ALPHA * x_ref[...] + f, lg_ref[...], lb_ref[...])


def _moe_combine(xf, gate, y, ln_g, ln_b):
    n, d = xf.shape
    tm = min(ROW_TILE, n)
    tiles = n // tm
    row_tile = pl.BlockSpec((tm, d), lambda i: (i, 0))
    slot_tile = lambda j: pl.BlockSpec((tm, d), lambda i: (j * tiles + i, 0))
    vec = pl.BlockSpec((1, d), lambda i: (0, 0))
    return pl.pallas_call(
        _combine_kernel,
        grid=(tiles,),
        in_specs=[row_tile, pl.BlockSpec((tm, TOP_K), lambda i: (i, 0))] + [slot_tile(j) for j in range(TOP_K)]
                 + [vec, vec],
        out_specs=row_tile,
        out_shape=jax.ShapeDtypeStruct((n, d), F32),
        compiler_params=_cparams(("parallel",)),
        name="moe_combine",
    )(xf, gate, y, y, y, y, ln_g.reshape(1, d), ln_b.reshape(1, d))


def _moe(xf, w_router, b_router, w1g, w1l, b1, w2, b2, ln_g, ln_b):
    n, d = xf.shape
    n_exp = w_router.shape[1]
    rows = min(EXPERT_ROWS, n)
    top_idx, gate = _router(xf, w_router, b_router)
    src, dst, block_expert, n_valid, n_used = _group_rows(top_idx, n_exp, rows)
    y = _moe_experts(xf, src, dst, block_expert, n_valid, n_used, w1g, w1l, b1[:, None, 0::2],
                     b1[:, None, 1::2], w2.astype(BF16), b2[:, None, :])
    return _moe_combine(xf, gate, y, ln_g, ln_b)


def kernel(x, rw_mix, rw_w_rkv, rw_w0, rw_w1, rw_w2, rw_a0, rw_a1, rw_a2, rw_g1, rw_g2, rw_k_k, rw_k_a, rw_r_k,
           rw_lnx_g, rw_lnx_b, rw_w_o, sb_w_qkv, sb_w_o, moe_w_router, moe_b_router, moe_w1, moe_b1, moe_w2,
           moe_b2, ln_g, ln_b):
    batch, seq_len, d = x.shape
    xf = x.reshape(batch * seq_len, d)
    n_layers, n_exp = moe_w1.shape[:2]
    w1g, w1l = _swiglu_split(moe_w1.reshape((n_layers * n_exp,) + moe_w1.shape[2:]))
    w1g = w1g.reshape((n_layers, n_exp) + w1g.shape[1:])
    w1l = w1l.reshape((n_layers, n_exp) + w1l.shape[1:])
    for i in range(DEPTH):
        j = i // 2
        if i % 2 == 0:
            r, k, v, ld, a, g = _rwkv_proj(xf, seq_len, rw_mix[j], rw_w_rkv[j], rw_w0[j], rw_w1[j], rw_w2[j],
                                           rw_a0[j], rw_a1[j], rw_a2[j], rw_g1[j], rw_g2[j])
            o = _rwkv_scan(r, k, v, ld, a, seq_len, rw_k_k[j], rw_k_a[j], rw_r_k[j], rw_lnx_g[j], rw_lnx_b[j])
            xf = _proj_ln(xf, o, g, rw_w_o[j], ln_g[i, 0], ln_b[i, 0])
        else:
            q, k, v = _sb_qkv(xf, sb_w_qkv[j])
            o = _sb_attn(q, k, v, seq_len)
            xf = _proj_ln(xf, o, None, sb_w_o[j], ln_g[i, 0], ln_b[i, 0])
        xf = _moe(xf, moe_w_router[i], moe_b_router[i], w1g[i], w1l[i], moe_b1[i], moe_w2[i], moe_b2[i],
                  ln_g[i, 1], ln_b[i, 1])
    return xf.reshape(batch, seq_len, d)
```

```python
import functools

import jax
import jax.numpy as jnp
from jax import lax
from jax.experimental import pallas as pl
from jax.experimental.pallas import tpu as pltpu

F32 = jnp.float32
BF16 = jnp.bfloat16
HIGHEST = lax.Precision.HIGHEST

DEPTH = 2
ALPHA = (2 * DEPTH) ** 0.25
LN_EPS = 1e-5
GN_EPS = 64e-5
HEAD_DIM = 64
LANES = 128
TOP_K = 4
SWIGLU_LIMIT = 7.0
SWIGLU_ALPHA = 1.702

ROW_TILE = 256
SCAN_CHUNK = 64
SCAN_ROWS = 1024
SCAN_UNROLL = 16
SB_Q_BLOCK = 256
SB_K_BLOCK = LANES
SB_GROUP = 2
SB_EXP_ZERO = -104.0
SB_ROW_CHUNK = 64
EXPERT_ROWS = 256
DMA_UNROLL = 8
DMA_PIECES = 4
SPLIT_COLS = 512
VMEM_LIMIT = 56 * 1024 * 1024


def _cparams(semantics):
    return pltpu.CompilerParams(dimension_semantics=semantics, vmem_limit_bytes=VMEM_LIMIT)


def _dot(a, b, precision=None):
    return jnp.dot(a, b, preferred_element_type=F32, precision=precision)


def _dot_nt(a, b, precision=None):
    return lax.dot_general(a, b, (((1,), (1,)), ((), ())), preferred_element_type=F32, precision=precision)


def _dot_tn(a, b, precision=None):
    return lax.dot_general(a, b, (((0,), (0,)), ((), ())), preferred_element_type=F32, precision=precision)


def _split2(a):
    hi = a.astype(BF16)
    return hi, (a - hi.astype(F32)).astype(BF16)


def _split3(a):
    hi = a.astype(BF16)
    rest = a - hi.astype(F32)
    mid = rest.astype(BF16)
    return hi, mid, (rest - mid.astype(F32)).astype(BF16)


def _mm3(a, b, dot=_dot):
    a_axis = 0 if dot is _dot_tn else 1
    b_axis = 1 if dot is _dot_nt else 0
    return dot(jnp.concatenate([a[0], a[0], a[1]], axis=a_axis), jnp.concatenate([b[0], b[1], b[0]], axis=b_axis))


def _softplus(u):
    return jnp.maximum(u, 0.0) + jnp.log(1.0 + jnp.exp(-jnp.abs(u)))


def _sigmoid(u):
    return 1.0 / (1.0 + jnp.exp(-u))


def _layer_norm(y, g, b):
    mu = jnp.mean(y, axis=-1, keepdims=True)
    d = y - mu
    var = jnp.mean(d * d, axis=-1, keepdims=True)
    return d * lax.rsqrt(var + LN_EPS) * g + b


def _head_block_ones():
    r = lax.broadcasted_iota(jnp.int32, (LANES, LANES), 0) // HEAD_DIM
    c = lax.broadcasted_iota(jnp.int32, (LANES, LANES), 1) // HEAD_DIM
    return (r == c).astype(BF16)


def _head_sums(x, head_ones):
    m = x.shape[0]
    s = _dot(jnp.concatenate(_split3(x), axis=0), head_ones)
    return s[:m] + s[m:2 * m] + s[2 * m:]


def _rwkv_proj_kernel(x_ref, xp_ref, mix_ref, wrkv_ref, w0_ref, w1_ref, w2_ref, a0_ref, a1_ref, a2_ref,
                      g1_ref, g2_ref, r_ref, k_ref, v_ref, ld_ref, a_ref, g_ref, *, tiles_per_seq):
    i = pl.program_id(0)
    x = x_ref[...]
    prev_last = jnp.where(i % tiles_per_seq == 0, 0.0, xp_ref[7:8, :])
    row = lax.broadcasted_iota(jnp.int32, x.shape, 0)
    x_prev = jnp.where(row == 0, prev_last, pltpu.roll(x, 1, 0))
    xx = x_prev - x

    def mixed(j):
        return (x + xx * mix_ref[j:j + 1, :]).astype(BF16)

    r_ref[...] = _dot(mixed(0), wrkv_ref[0])
    w = w0_ref[...] + _dot(jnp.tanh(_dot(mixed(1), w1_ref[...])).astype(BF16), w2_ref[...])
    ld_ref[...] = -jnp.exp(-_softplus(-w) - 0.5)
    k_ref[...] = _dot(mixed(2), wrkv_ref[1])
    v_ref[...] = _dot(mixed(3), wrkv_ref[2])
    a_ref[...] = _sigmoid(a0_ref[...] + _dot(_dot(mixed(4), a1_ref[...]).astype(BF16), a2_ref[...]))
    g_ref[...] = _dot(_sigmoid(_dot(mixed(5), g1_ref[...])).astype(BF16), g2_ref[...])


def _rwkv_proj(xf, seq_len, mix, w_rkv, w0, w1, w2, a0, a1, a2, g1, g2):
    n, d = xf.shape
    tm = min(ROW_TILE, seq_len)
    full = lambda shape: pl.BlockSpec(shape, lambda i: (0,) * len(shape))
    row_tile = pl.BlockSpec((tm, d), lambda i: (i, 0))
    prev_rows = pl.BlockSpec((8, d), lambda i: (jnp.maximum(i * (tm // 8) - 1, 0), 0))
    mix8 = jnp.concatenate([mix, jnp.zeros((2, d), F32)], axis=0)
    out = jax.ShapeDtypeStruct((n, d), F32)
    return pl.pallas_call(
        functools.partial(_rwkv_proj_kernel, tiles_per_seq=seq_len // tm),
        grid=(n // tm,),
        in_specs=[row_tile, prev_rows, full((8, d)), full(w_rkv.shape), full((1, d)), full(w1.shape),
                  full(w2.shape), full((1, d)), full(a1.shape), full(a2.shape), full(g1.shape), full(g2.shape)],
        out_specs=[row_tile] * 6,
        out_shape=[out] * 6,
        compiler_params=_cparams(("parallel",)),
        name="rwkv_proj",
    )(xf, xf, mix8, w_rkv.astype(BF16), w0.reshape(1, d), w1.astype(BF16), w2.astype(BF16),
      a0.reshape(1, d), a1.astype(BF16), a2.astype(BF16), g1.astype(BF16), g2.astype(BF16))


def _rwkv_scan_kernel(r_ref, k_ref, v_ref, ld_ref, a_ref, kk_ref, ka_ref, rk_ref, lg_ref, lb_ref,
                      o_ref, s_ref, rhat_ref, o0_ref, p_ref, q_ref, dcol_ref, *, chunk, n_chunks, unroll):
    L = chunk

    @pl.when(pl.program_id(1) == 0)
    def _():
        s_ref[...] = jnp.zeros_like(s_ref)

    head0 = lax.broadcasted_iota(jnp.int32, (1, LANES), 1) < HEAD_DIM
    row = lax.broadcasted_iota(jnp.int32, (2 * L, 2 * L), 0)
    col = lax.broadcasted_iota(jnp.int32, (2 * L, 2 * L), 1)
    same_head = (row // L) == (col // L)
    lower_strict = jnp.logical_and(same_head, col < row)
    lower_incl = jnp.logical_and(same_head, col <= row)
    eye = (row == col).astype(F32)
    cum_rows = (lax.broadcasted_iota(jnp.int32, (L, L), 1)
                <= lax.broadcasted_iota(jnp.int32, (L, L), 0)).astype(BF16)
    head_ones = _head_block_ones()
    ones_ln = jnp.ones((L, LANES), BF16)

    def stack(t):
        return jnp.concatenate([jnp.where(head0, t, 0.0), jnp.where(head0, 0.0, t)], axis=0)

    def modified_key(k, a):
        return k * (1.0 + (a - 1.0) * ka_ref[...])

    def prepare(ci):
        sl = pl.ds(pl.multiple_of(ci * L, L), L)
        r, k, v, ld, a = r_ref[sl, :], k_ref[sl, :], v_ref[sl, :], ld_ref[sl, :], a_ref[sl, :]
        kk = k * kk_ref[...]
        kk_sq = _head_sums(kk * kk, head_ones)
        ld3 = jnp.concatenate(_split3(ld), axis=1)
        c3 = _dot(cum_rows, ld3)
        tot3 = _dot_tn(ld3, ones_ln)
        yield
        kk = kk / jnp.maximum(jnp.sqrt(kk_sq), 1e-12)
        k2 = modified_key(k, a)
        a_vec, b_vec = -kk, kk * a
        c = c3[:, :LANES] + c3[:, LANES:2 * LANES] + c3[:, 2 * LANES:]
        dcol_ref[ci] = jnp.exp(tot3[:LANES] + tot3[LANES:2 * LANES] + tot3[2 * LANES:])
        to_end = jnp.exp(c[L - 1:L, :] - c)
        inv = jnp.exp(-c)
        a_st = stack(a_vec * jnp.exp(c - ld))
        r_st = stack(r * jnp.exp(c))
        v_sp = _split2(stack(v))
        bh_sp = _split2(stack(b_vec * to_end))
        kh_sp = _split2(stack(k2 * to_end))
        ar = _split2(jnp.concatenate([a_st, r_st], axis=0))
        bk = _split2(jnp.concatenate([stack(b_vec * inv), stack(k2 * inv)], axis=0))
        prod = _mm3(ar, bk, _dot_nt)
        kv = _mm3(kh_sp, v_sp, _dot_tn)
        yield
        m_ab = jnp.where(lower_strict, prod[:2 * L, :2 * L], 0.0)
        m_ak = jnp.where(lower_strict, prod[:2 * L, 2 * L:], 0.0)
        a_rb = jnp.where(lower_incl, prod[2 * L:, :2 * L], 0.0)
        a_rk = jnp.where(lower_incl, prod[2 * L:, 2 * L:], 0.0)
        w0 = _mm3(_split2(m_ak), v_sp)
        rkv = _mm3(_split2(a_rk), v_sp)
        t_inv = eye
        w = 1
        while w < L:
            below = jnp.logical_and((row // (2 * w)) == (col // (2 * w)),
                                    jnp.logical_and((row % (2 * w)) >= w, (col % (2 * w)) < w))
            t_bf = t_inv.astype(BF16)
            tm = _dot(t_bf, jnp.where(below, m_ab, 0.0).astype(BF16))
            yield
            tmt = _dot(tm.astype(BF16), t_bf)
            yield
            t_inv = t_inv + tmt
            w *= 2
        resid = (eye - t_inv) + _mm3(_split2(m_ab), _split2(t_inv))
        yield
        t_inv = t_inv + _dot(t_inv.astype(BF16), resid.astype(BF16))
        yield
        au = _mm3(_split2(t_inv), _split2(jnp.concatenate([a_st, w0], axis=1)))
        yield
        au_sp = _split2(au)
        ro = _mm3(_split2(a_rb), au_sp)
        pq = _mm3(bh_sp, au_sp, _dot_tn)
        yield
        rhat_ref[ci] = r_st + ro[:, :LANES]
        o0_ref[ci] = ro[:, LANES:] + rkv
        p_ref[ci] = pq[:, :LANES]
        q_ref[ci] = pq[:, LANES:] + kv

    def prepare_group(gi, carry):
        chunks = [prepare(gi * unroll + u) for u in range(unroll)]
        while chunks:
            for g in chunks:
                if next(g, StopIteration) is StopIteration:
                    chunks = []
        return carry

    lax.fori_loop(0, n_chunks // unroll, prepare_group, 0)

    s = s_ref[...]
    for ci in range(n_chunks):
        s_sp = _split2(s)
        o_st = _mm3(_split2(rhat_ref[ci]), s_sp) + o0_ref[ci]
        o_ref[ci * L:(ci + 1) * L, :] = o_st[:L] + o_st[L:]
        s = dcol_ref[ci] * s + _mm3(_split2(p_ref[ci]), s_sp) + q_ref[ci]
    s_ref[...] = s

    o = o_ref[...]
    inv_n = 1.0 / HEAD_DIM
    mu = _head_sums(o, head_ones) * inv_n
    d = o - mu
    var = _head_sums(d * d, head_ones) * inv_n
    r, v = r_ref[...], v_ref[...]
    k2 = modified_key(k_ref[...], a_ref[...])
    bonus = _head_sums(r * k2 * rk_ref[...], head_ones) * v
    o_ref[...] = d * lax.rsqrt(var + GN_EPS) * lg_ref[...] + lb_ref[...] + bonus


def _rwkv_scan(r, k, v, ld, a, seq_len, k_k, k_a, r_k, lnx_g, lnx_b):
    n, d = r.shape
    batch = n // seq_len
    pairs = d // LANES
    rows = min(SCAN_ROWS, seq_len)
    steps = seq_len // rows
    n_chunks = rows // SCAN_CHUNK
    seq = pl.BlockSpec((rows, LANES), lambda p, c: ((p // pairs) * steps + c, p % pairs))
    par = pl.BlockSpec((1, LANES), lambda p, c: (0, p % pairs))
    per_chunk = pltpu.VMEM((n_chunks, 2 * SCAN_CHUNK, LANES), F32)
    return pl.pallas_call(
        functools.partial(_rwkv_scan_kernel, chunk=SCAN_CHUNK, n_chunks=n_chunks,
                          unroll=min(SCAN_UNROLL, n_chunks)),
        grid=(batch * pairs, steps),
        in_specs=[seq] * 5 + [par] * 5,
        out_specs=seq,
        out_shape=jax.ShapeDtypeStruct((n, d), F32),
        scratch_shapes=[pltpu.VMEM((LANES, LANES), F32)] + [per_chunk] * 5,
        compiler_params=_cparams(("parallel", "arbitrary")),
        name="rwkv_scan",
    )(r, k, v, ld, a, k_k.reshape(1, d), k_a.reshape(1, d), r_k.reshape(1, d),
      lnx_g.reshape(1, d), lnx_b.reshape(1, d))


def _proj_ln_kernel(*refs, gated):
    if gated:
        x_ref, o_ref, g_ref, w_ref, lg_ref, lb_ref, out_ref = refs
        h = o_ref[...] * g_ref[...]
    else:
        x_ref, o_ref, w_ref, lg_ref, lb_ref, out_ref = refs
        h = o_ref[...]
    y = ALPHA * x_ref[...] + _dot(h.astype(BF16), w_ref[...])
    out_ref[...] = _layer_norm(y, lg_ref[...], lb_ref[...])


def _proj_ln(xf, o, g, w_o, ln_g, ln_b):
    n, d = xf.shape
    tm = min(ROW_TILE, n)
    row_tile = pl.BlockSpec((tm, d), lambda i: (i, 0))
    full = lambda shape: pl.BlockSpec(shape, lambda i: (0,) * len(shape))
    gated = g is not None
    acts = (xf, o, g) if gated else (xf, o)
    return pl.pallas_call(
        functools.partial(_proj_ln_kernel, gated=gated),
        grid=(n // tm,),
        in_specs=[row_tile] * len(acts) + [full((d, d)), full((1, d)), full((1, d))],
        out_specs=row_tile,
        out_shape=jax.ShapeDtypeStruct((n, d), F32),
        compiler_params=_cparams(("parallel",)),
        name="proj_ln",
    )(*acts, w_o.astype(BF16), ln_g.reshape(1, d), ln_b.reshape(1, d))


def _sb_qkv_kernel(x_ref, w_ref, q_ref, k_ref, v_ref):
    d = x_ref.shape[1]
    qkv = _dot(x_ref[...].astype(BF16), w_ref[...])
    q_ref[...] = (qkv[:, :d] * (HEAD_DIM ** -0.5)).astype(BF16)
    k_ref[...] = qkv[:, d:2 * d].astype(BF16)
    v_ref[...] = qkv[:, 2 * d:].astype(BF16)


def _sb_qkv(xf, w_qkv):
    n, d = xf.shape
    tm = min(ROW_TILE, n)
    row_tile = pl.BlockSpec((tm, d), lambda i: (i, 0))
    out = jax.ShapeDtypeStruct((n, d), BF16)
    return pl.pallas_call(
        _sb_qkv_kernel,
        grid=(n // tm,),
        in_specs=[row_tile, pl.BlockSpec((d, 3 * d), lambda i: (0, 0))],
        out_specs=[row_tile] * 3,
        out_shape=[out] * 3,
        compiler_params=_cparams(("parallel",)),
        name="sb_qkv",
    )(xf, w_qkv.astype(BF16))


def _sb_attn_kernel(q_ref, k_ref, v_ref, o_ref, acc_ref, run_ref, ls_buf, lk_buf, sum_buf, att_buf,
                    *, bq, bk, group):
    i = pl.program_id(1)
    ratio = bq // bk
    m = 2 * bq
    head0 = lax.broadcasted_iota(jnp.int32, (1, LANES), 1) < HEAD_DIM
    q = q_ref[...]
    zero = jnp.zeros_like(q)
    q2 = jnp.concatenate([jnp.where(head0, q, zero), jnp.where(head0, zero, q)], axis=0)
    kr = lax.broadcasted_iota(jnp.int32, (2 * bk, 2 * bk), 0) % bk
    kc = lax.broadcasted_iota(jnp.int32, (2 * bk, 2 * bk), 1)
    suffix = jnp.logical_or(kr > kc, kc >= bk).astype(BF16)
    acc_ref[...] = jnp.zeros_like(acc_ref)
    run_ref[...] = jnp.zeros_like(run_ref)
    row_chunks = [slice(c, c + SB_ROW_CHUNK) for c in range(0, m, SB_ROW_CHUNK)]

    def causal(j, rs):
        q_pos = i * bq + (lax.broadcasted_iota(jnp.int32, (SB_ROW_CHUNK, bk), 0) + rs.start) % bq
        return (j * bk + lax.broadcasted_iota(jnp.int32, (SB_ROW_CHUNK, bk), 1)) < q_pos

    def stages(j, masked, u):
        ks = pl.ds(pl.multiple_of(j * bk, bk), bk)

        def scores():
            ls_buf[u] = _dot_nt(q2, k_ref[ks, :])

        def log_terms():
            for rs in row_chunks:
                z = ls_buf[u, rs, :]
                log_keep = -_softplus(z)
                if masked:
                    log_keep = jnp.where(causal(j, rs), log_keep, 0.0)
                hi, lo = _split2(log_keep)
                lk_buf[u, rs, :bk] = hi
                lk_buf[u, rs, bk:] = lo
                ls_buf[u, rs, :] = z + log_keep

        def suffix_sums():
            sum_buf[u] = _dot(lk_buf[u], suffix)

        def weights():
            for rs in row_chunks:
                run = run_ref[rs, :]
                att = jnp.exp(ls_buf[u, rs, :] + run + sum_buf[u, rs, :bk])
                if masked:
                    att = jnp.where(causal(j, rs), att, 0.0)
                att_buf[u, rs, :] = att.astype(BF16)
                run_ref[rs, :] = run + sum_buf[u, rs, bk:]

        def output():
            acc_ref[...] += _dot(att_buf[u], v_ref[ks, :])

        return (scores, log_terms, suffix_sums, weights, output)

    def run_tiles(tiles, masked):
        per_tile = [stages(j, masked, u) for u, j in enumerate(tiles)]
        for stage in zip(*per_tile):
            for emit in stage:
                emit()

    run_tiles([(i + 1) * ratio - 1 - u for u in range(ratio)], True)
    n_off = i * ratio

    def live(step):
        return jnp.logical_and(step < n_off // group, jnp.max(run_ref[...]) > SB_EXP_ZERO)

    def body(step):
        run_tiles([n_off - 1 - step * group - u for u in range(group)], False)
        return step + 1

    lax.while_loop(live, body, 0)
    o_ref[...] = jnp.where(head0, acc_ref[:bq, :], acc_ref[bq:, :])


def _sb_attn(q, k, v, seq_len):
    n, d = q.shape
    batch = n // seq_len
    pairs = d // LANES
    bq = min(SB_Q_BLOCK, seq_len)
    bk = SB_K_BLOCK
    nq = seq_len // bq
    ratio = bq // bk
    group = min(SB_GROUP, ratio)
    assert ratio % group == 0
    in_flight = ratio
    q_spec = pl.BlockSpec((bq, LANES), lambda p, i: ((p // pairs) * nq + i, p % pairs))
    kv_spec = pl.BlockSpec((seq_len, LANES), lambda p, i: (p // pairs, p % pairs))
    return pl.pallas_call(
        functools.partial(_sb_attn_kernel, bq=bq, bk=bk, group=group),
        grid=(batch * pairs, nq),
        in_specs=[q_spec, kv_spec, kv_spec],
        out_specs=q_spec,
        out_shape=jax.ShapeDtypeStruct((n, d), F32),
        scratch_shapes=[pltpu.VMEM((2 * bq, LANES), F32), pltpu.VMEM((2 * bq, LANES), F32),
                        pltpu.VMEM((in_flight, 2 * bq, bk), F32), pltpu.VMEM((in_flight, 2 * bq, 2 * bk), BF16),
                        pltpu.VMEM((in_flight, 2 * bq, 2 * bk), F32), pltpu.VMEM((in_flight, 2 * bq, bk), BF16)],
        compiler_params=_cparams(("parallel", "arbitrary")),
        name="sb_attn",
    )(q, k, v)


def _router_kernel(x_ref, w_ref, b_ref, idx_ref, gate_ref):
    logits = _dot_nt(w_ref[...], x_ref[...], HIGHEST) + b_ref[...]
    n_exp = logits.shape[0]
    e_idx = lax.broadcasted_iota(jnp.int32, logits.shape, 0)
    vals, idxs = [], []
    work = logits
    for _ in range(TOP_K):
        m = jnp.max(work, axis=0, keepdims=True)
        am = jnp.min(jnp.where(work == m, e_idx, n_exp), axis=0, keepdims=True)
        vals.append(m)
        idxs.append(am)
        work = jnp.where(e_idx == am, -jnp.inf, work)
    ex = [jnp.exp(vj - vals[0]) for vj in vals]
    inv = 1.0 / (ex[0] + ex[1] + ex[2] + ex[3])
    pad_i = jnp.zeros((8 - TOP_K, logits.shape[1]), jnp.int32)
    pad_f = jnp.zeros((8 - TOP_K, logits.shape[1]), F32)
    idx_ref[...] = jnp.concatenate(idxs + [pad_i], axis=0)
    gate_ref[...] = jnp.concatenate([e * inv for e in ex] + [pad_f], axis=0)


def _router(xf, w_router, b_router):
    n, d = xf.shape
    n_exp = w_router.shape[1]
    tm = min(ROW_TILE, n)
    out_spec = pl.BlockSpec((8, tm), lambda i: (0, i))
    idx, gate = pl.pallas_call(
        _router_kernel,
        grid=(n // tm,),
        in_specs=[pl.BlockSpec((tm, d), lambda i: (i, 0)), pl.BlockSpec((n_exp, d), lambda i: (0, 0)),
                  pl.BlockSpec((n_exp, 1), lambda i: (0, 0))],
        out_specs=[out_spec, out_spec],
        out_shape=[jax.ShapeDtypeStruct((8, n), jnp.int32), jax.ShapeDtypeStruct((8, n), F32)],
        compiler_params=_cparams(("parallel",)),
        name="moe_router",
    )(xf, w_router.T, b_router.reshape(n_exp, 1))
    return idx[:TOP_K].T, gate[:TOP_K].T


def _group_rows(top_idx, n_exp, rows):
    n = top_idx.shape[0]
    n_assign = n * TOP_K
    expert = top_idx.reshape(-1)
    order = jnp.argsort(expert, stable=True).astype(jnp.int32)
    counts = jnp.sum(expert[:, None] == jnp.arange(n_exp, dtype=jnp.int32)[None, :], axis=0, dtype=jnp.int32)
    padded = (counts + rows - 1) // rows * rows
    pad_end = jnp.cumsum(padded)
    pad_start = pad_end - padded
    n_blocks = n_assign // rows + n_exp
    block_pos = jnp.arange(n_blocks, dtype=jnp.int32)[:, None] * rows
    block_expert = jnp.minimum(jnp.sum(block_pos >= pad_end[None, :], axis=1, dtype=jnp.int32), n_exp - 1)
    pos = jnp.arange(n_blocks * rows, dtype=jnp.int32)
    blk = pos // rows
    past = pos[:, None] >= pad_end[None, :]
    n_past = jnp.sum(past, axis=1, dtype=jnp.int32)
    gap = jnp.sum(jnp.where(past, (padded - counts)[None, :], 0), axis=1, dtype=jnp.int32)
    n_real_end = jnp.sum(pos[:, None] >= (pad_start + counts)[None, :], axis=1, dtype=jnp.int32)
    valid = jnp.logical_and(n_real_end == n_past, pos < pad_end[-1])
    assign = order[jnp.clip(pos - gap, 0, n_assign - 1)]
    tok = assign // TOP_K
    slot = assign % TOP_K
    src = jnp.where(valid, tok, 0).reshape(n_blocks, 1, rows)
    dump = n_assign + (blk % 2) * rows + pos % rows
    dst = jnp.where(valid, slot * n + tok, dump).reshape(n_blocks, 1, rows)
    before_first = (n_assign + rows + jnp.arange(rows, dtype=jnp.int32)).reshape(1, 1, rows)
    src = jnp.concatenate([src, jnp.zeros((1, 1, rows), jnp.int32)], axis=0)
    dst_prev = jnp.concatenate([before_first, dst], axis=0)
    block_expert = jnp.concatenate([block_expert, block_expert[-1:]])
    n_used = (pad_end[-1] // rows).astype(jnp.int32).reshape(1)
    return src, dst_prev, block_expert, n_used


def _swiglu_split_kernel(w_ref, even_ref, odd_ref, glu_ref, lin_ref):
    w = w_ref[0].astype(BF16)
    glu_ref[0] = _dot(w, even_ref[...]).astype(BF16)
    lin_ref[0] = _dot(w, odd_ref[...]).astype(BF16)


def _swiglu_split(w1):
    n_exp, d, f2 = w1.shape
    cols = min(SPLIT_COLS, f2)
    r = jnp.arange(cols, dtype=jnp.int32)[:, None]
    c = jnp.arange(cols // 2, dtype=jnp.int32)[None, :]
    even = (r == 2 * c).astype(BF16)
    odd = (r == 2 * c + 1).astype(BF16)
    sel = pl.BlockSpec((cols, cols // 2), lambda e, j: (0, 0))
    out_spec = pl.BlockSpec((1, d, cols // 2), lambda e, j: (e, 0, j))
    out = jax.ShapeDtypeStruct((n_exp, d, f2 // 2), BF16)
    return pl.pallas_call(
        _swiglu_split_kernel,
        grid=(n_exp, f2 // cols),
        in_specs=[pl.BlockSpec((1, d, cols), lambda e, j: (e, 0, j)), sel, sel],
        out_specs=[out_spec, out_spec],
        out_shape=[out, out],
        compiler_params=_cparams(("parallel", "parallel")),
        name="swiglu_split",
    )(w1, even, odd)


def _expert_kernel(bexp_ref, nused_ref, src0_ref, src_next_ref, dst_prev_ref, x_hbm, w1g_ref, w1l_ref,
                   b1g_ref, b1l_ref, w2_ref, b2_ref, y_hbm, xbuf, ybuf, gsem, ssem, isem, *, rows, n_tok_rows):
    del bexp_ref
    b = pl.program_id(0)
    n_used = nused_ref[0]
    slot = b % 2
    other = 1 - slot

    def gather_start(idx_ref, s, i):
        pltpu.make_async_copy(x_hbm.at[pl.ds(idx_ref[0, 0, i], 1), :], xbuf.at[s, pl.ds(i, 1), :],
                              gsem.at[s]).start()

    def scatter_start(s, i):
        pltpu.make_async_copy(ybuf.at[s, pl.ds(i, 1), :], y_hbm.at[pl.ds(dst_prev_ref[0, 0, i], 1), :],
                              ssem.at[s]).start()

    def wait_gather(s):
        pltpu.make_async_copy(x_hbm.at[pl.ds(0, rows), :], xbuf.at[s], gsem.at[s]).wait()

    def wait_scatter(s):
        pltpu.make_async_copy(ybuf.at[s], y_hbm.at[pl.ds(0, rows), :], ssem.at[s]).wait()

    @pl.when(b == 0)
    def _():
        ybuf[...] = jnp.zeros_like(ybuf)
        for half in range(2):
            fill = pltpu.make_async_copy(ybuf.at[half], y_hbm.at[pl.ds(n_tok_rows + half * rows, rows), :],
                                         isem.at[0])
            fill.start()
            fill.wait()

        def first(i, carry):
            gather_start(src0_ref, 0, i)
            return carry
        lax.fori_loop(0, rows, first, 0, unroll=DMA_UNROLL)

    @pl.when(jnp.logical_and(b >= 1, b <= n_used))
    def _():
        wait_scatter(slot)

    @pl.when(b <= n_used)
    def _():
        wait_gather(slot)

    @pl.when(b < n_used)
    def _():
        per_piece = rows // DMA_PIECES

        def copies(piece):
            for i in range(piece * per_piece, (piece + 1) * per_piece):
                gather_start(src_next_ref, other, i)
                scatter_start(other, i)

        xb = xbuf[slot].astype(BF16)
        copies(0)
        glu = jnp.minimum(_dot(xb, w1g_ref[0]) + b1g_ref[0], SWIGLU_LIMIT)
        copies(1)
        lin = jnp.clip(_dot(xb, w1l_ref[0]) + b1l_ref[0], -SWIGLU_LIMIT, SWIGLU_LIMIT)
        copies(2)
        act = glu * _sigmoid(SWIGLU_ALPHA * glu) * (lin + 1.0)
        copies(3)
        ybuf[slot] = _dot(act.astype(BF16), w2_ref[0]) + b2_ref[0]

    @pl.when(b == n_used)
    def _():
        def last(i, carry):
            scatter_start(other, i)
            return carry
        lax.fori_loop(0, rows, last, 0, unroll=DMA_UNROLL)
        wait_scatter(other)


def _moe_experts(xf, src, dst_prev, block_expert, n_used, w1g, w1l, b1g, b1l, w2, b2):
    n, d = xf.shape
    n_steps, _, rows = src.shape
    n_exp, _, f = w1g.shape
    idx_spec = lambda index_map: pl.BlockSpec((1, 1, rows), index_map, memory_space=pltpu.SMEM)
    per_expert = lambda shape: pl.BlockSpec((1,) + shape, lambda b, be, nu: (be[b], 0, 0))
    grid_spec = pltpu.PrefetchScalarGridSpec(
        num_scalar_prefetch=2,
        grid=(n_steps,),
        in_specs=[idx_spec(lambda b, be, nu: (0, 0, 0)),
                  idx_spec(lambda b, be, nu: (jnp.minimum(b + 1, n_steps - 1), 0, 0)),
                  idx_spec(lambda b, be, nu: (b, 0, 0)), pl.BlockSpec(memory_space=pl.ANY),
                  per_expert((d, f)), per_expert((d, f)), per_expert((1, f)), per_expert((1, f)),
                  per_expert((f, d)), per_expert((1, d))],
        out_specs=pl.BlockSpec(memory_space=pl.ANY),
        scratch_shapes=[pltpu.VMEM((2, rows, d), F32), pltpu.VMEM((2, rows, d), F32),
                        pltpu.SemaphoreType.DMA((2,)), pltpu.SemaphoreType.DMA((2,)),
                        pltpu.SemaphoreType.DMA((1,))],
    )
    return pl.pallas_call(
        functools.partial(_expert_kernel, rows=rows, n_tok_rows=TOP_K * n),
        grid_spec=grid_spec,
        out_shape=jax.ShapeDtypeStruct((TOP_K * n + 2 * rows, d), F32),
        compiler_params=_cparams(("arbitrary",)),
        name="moe_experts",
    )(block_expert, n_used, src, src, dst_prev, xf, w1g, w1l, b1g, b1l, w2, b2)


def _combine_kernel(x_ref, gate_ref, y0_ref, y1_ref, y2_ref, y3_ref, lg_ref, lb_ref, out_ref):
    gate = gate_ref[...]
    f = (gate[:, 0:1] * y0_ref[...] + gate[:, 1:2] * y1_ref[...]
         + gate[:, 2:3] * y2_ref[...] + gate[:, 3:4] * y3_ref[...])
    out_ref[...] = _layer_norm(ALPHA * x_ref[...] + f, lg_ref[...], lb_ref[...])


def _moe_combine(xf, gate, y, ln_g, ln_b):
    n, d = xf.shape
    tm = min(ROW_TILE, n)
    tiles = n // tm
    row_tile = pl.BlockSpec((tm, d), lambda i: (i, 0))
    slot_tile = lambda j: pl.BlockSpec((tm, d), lambda i: (j * tiles + i, 0))
    vec = pl.BlockSpec((1, d), lambda i: (0, 0))
    return pl.pallas_call(
        _combine_kernel,
        grid=(tiles,),
        in_specs=[row_tile, pl.BlockSpec((tm, TOP_K), lambda i: (i, 0))] + [slot_tile(j) for j in range(TOP_K)]
                 + [vec, vec],
        out_specs=row_tile,
        out_shape=jax.ShapeDtypeStruct((n, d), F32),
        compiler_params=_cparams(("parallel",)),
        name="moe_combine",
    )(xf, gate, y, y, y, y, ln_g.reshape(1, d), ln_b.reshape(1, d))


def _moe(xf, w_router, b_router, w1g, w1l, b1, w2, b2, ln_g, ln_b):
    n, d = xf.shape
    n_exp = w_router.shape[1]
    rows = min(EXPERT_ROWS, n)
    top_idx, gate = _router(xf, w_router, b_router)
    src, dst_prev, block_expert, n_used = _group_rows(top_idx, n_exp, rows)
    y = _moe_experts(xf, src, dst_prev, block_expert, n_used, w1g, w1l, b1[:, None, 0::2], b1[:, None, 1::2],
                     w2.astype(BF16), b2[:, None, :])
    return _moe_combine(xf, gate, y, ln_g, ln_b)


def kernel(x, rw_mix, rw_w_rkv, rw_w0, rw_w1, rw_w2, rw_a0, rw_a1, rw_a2, rw_g1, rw_g2, rw_k_k, rw_k_a, rw_r_k,
           rw_lnx_g, rw_lnx_b, rw_w_o, sb_w_qkv, sb_w_o, moe_w_router, moe_b_router, moe_w1, moe_b1, moe_w2,
           moe_b2, ln_g, ln_b):
    batch, seq_len, d = x.shape
    xf = x.reshape(batch * seq_len, d)
    n_layers, n_exp = moe_w1.shape[:2]
    w1g, w1l = _swiglu_split(moe_w1.reshape((n_layers * n_exp,) + moe_w1.shape[2:]))
    w1g = w1g.reshape((n_layers, n_exp) + w1g.shape[1:])
    w1l = w1l.reshape((n_layers, n_exp) + w1l.shape[1:])
    for i in range(DEPTH):
        j = i // 2
        if i % 2 == 0:
            r, k, v, ld, a, g = _rwkv_proj(xf, seq_len, rw_mix[j], rw_w_rkv[j], rw_w0[j], rw_w1[j], rw_w2[j],
                                           rw_a0[j], rw_a1[j], rw_a2[j], rw_g1[j], rw_g2[j])
            o = _rwkv_scan(r, k, v, ld, a, seq_len, rw_k_k[j], rw_k_a[j], rw_r_k[j], rw_lnx_g[j], rw_lnx_b[j])
            xf = _proj_ln(xf, o, g, rw_w_o[j], ln_g[i, 0], ln_b[i, 0])
        else:
            q, k, v = _sb_qkv(xf, sb_w_qkv[j])
            o = _sb_attn(q, k, v, seq_len)
            xf = _proj_ln(xf, o, None, sb_w_o[j], ln_g[i, 0], ln_b[i, 0])
        xf = _moe(xf, moe_w_router[i], moe_b_router[i], w1g[i], w1l[i], moe_b1[i], moe_w2[i], moe_b2[i],
                  ln_g[i, 1], ln_b[i, 1])
    return xf.reshape(batch, seq_len, d)
```

```python
import functools

import jax
import jax.numpy as jnp
from jax import lax
from jax.experimental import pallas as pl
from jax.experimental.pallas import tpu as pltpu

F32 = jnp.float32
BF16 = jnp.bfloat16
HIGHEST = lax.Precision.HIGHEST

DEPTH = 2
ALPHA = (2 * DEPTH) ** 0.25
LN_EPS = 1e-5
GN_EPS = 64e-5
HEAD_DIM = 64
LANES = 128
TOP_K = 4
SWIGLU_LIMIT = 7.0
SWIGLU_ALPHA = 1.702

ROW_TILE = 256
SCAN_CHUNK = 64
SCAN_ROWS = 1024
SCAN_UNROLL = 16
SB_Q_BLOCK = 256
SB_K_BLOCK = LANES
SB_GROUP = 2
SB_EXP_ZERO = -104.0
SB_ROW_CHUNK = 64
EXPERT_ROWS = 256
DMA_UNROLL = 8
DMA_PIECES = 4
SPLIT_COLS = 512
VMEM_LIMIT = 56 * 1024 * 1024


def _cparams(semantics):
    return pltpu.CompilerParams(dimension_semantics=semantics, vmem_limit_bytes=VMEM_LIMIT)


def _dot(a, b, precision=None):
    return jnp.dot(a, b, preferred_element_type=F32, precision=precision)


def _dot_nt(a, b, precision=None):
    return lax.dot_general(a, b, (((1,), (1,)), ((), ())), preferred_element_type=F32, precision=precision)


def _dot_tn(a, b, precision=None):
    return lax.dot_general(a, b, (((0,), (0,)), ((), ())), preferred_element_type=F32, precision=precision)


def _split2(a):
    hi = a.astype(BF16)
    return hi, (a - hi.astype(F32)).astype(BF16)


def _split3(a):
    hi = a.astype(BF16)
    rest = a - hi.astype(F32)
    mid = rest.astype(BF16)
    return hi, mid, (rest - mid.astype(F32)).astype(BF16)


def _mm3(a, b, dot=_dot):
    a_axis = 0 if dot is _dot_tn else 1
    b_axis = 1 if dot is _dot_nt else 0
    return dot(jnp.concatenate([a[0], a[0], a[1]], axis=a_axis), jnp.concatenate([b[0], b[1], b[0]], axis=b_axis))


def _softplus(u):
    return jnp.maximum(u, 0.0) + jnp.log(1.0 + jnp.exp(-jnp.abs(u)))


def _sigmoid(u):
    return 1.0 / (1.0 + jnp.exp(-u))


def _layer_norm(y, g, b):
    mu = jnp.mean(y, axis=-1, keepdims=True)
    d = y - mu
    var = jnp.mean(d * d, axis=-1, keepdims=True)
    return d * lax.rsqrt(var + LN_EPS) * g + b


def _head_block_ones():
    r = lax.broadcasted_iota(jnp.int32, (LANES, LANES), 0) // HEAD_DIM
    c = lax.broadcasted_iota(jnp.int32, (LANES, LANES), 1) // HEAD_DIM
    return (r == c).astype(BF16)


def _head_sums(x, head_ones):
    m = x.shape[0]
    s = _dot(jnp.concatenate(_split3(x), axis=0), head_ones)
    return s[:m] + s[m:2 * m] + s[2 * m:]


def _rwkv_proj_kernel(x_ref, xp_ref, mix_ref, wrkv_ref, w0_ref, w1_ref, w2_ref, a0_ref, a1_ref, a2_ref,
                      g1_ref, g2_ref, r_ref, k_ref, v_ref, ld_ref, a_ref, g_ref, *, tiles_per_seq):
    i = pl.program_id(0)
    x = x_ref[...]
    prev_last = jnp.where(i % tiles_per_seq == 0, 0.0, xp_ref[7:8, :])
    row = lax.broadcasted_iota(jnp.int32, x.shape, 0)
    x_prev = jnp.where(row == 0, prev_last, pltpu.roll(x, 1, 0))
    xx = x_prev - x

    def mixed(j):
        return (x + xx * mix_ref[j:j + 1, :]).astype(BF16)

    r_ref[...] = _dot(mixed(0), wrkv_ref[0])
    w = w0_ref[...] + _dot(jnp.tanh(_dot(mixed(1), w1_ref[...])).astype(BF16), w2_ref[...])
    ld_ref[...] = -jnp.exp(-_softplus(-w) - 0.5)
    k_ref[...] = _dot(mixed(2), wrkv_ref[1])
    v_ref[...] = _dot(mixed(3), wrkv_ref[2])
    a_ref[...] = _sigmoid(a0_ref[...] + _dot(_dot(mixed(4), a1_ref[...]).astype(BF16), a2_ref[...]))
    g_ref[...] = _dot(_sigmoid(_dot(mixed(5), g1_ref[...])).astype(BF16), g2_ref[...])


def _rwkv_proj(xf, seq_len, mix, w_rkv, w0, w1, w2, a0, a1, a2, g1, g2):
    n, d = xf.shape
    tm = min(ROW_TILE, seq_len)
    full = lambda shape: pl.BlockSpec(shape, lambda i: (0,) * len(shape))
    row_tile = pl.BlockSpec((tm, d), lambda i: (i, 0))
    prev_rows = pl.BlockSpec((8, d), lambda i: (jnp.maximum(i * (tm // 8) - 1, 0), 0))
    mix8 = jnp.concatenate([mix, jnp.zeros((2, d), F32)], axis=0)
    out = jax.ShapeDtypeStruct((n, d), F32)
    return pl.pallas_call(
        functools.partial(_rwkv_proj_kernel, tiles_per_seq=seq_len // tm),
        grid=(n // tm,),
        in_specs=[row_tile, prev_rows, full((8, d)), full(w_rkv.shape), full((1, d)), full(w1.shape),
                  full(w2.shape), full((1, d)), full(a1.shape), full(a2.shape), full(g1.shape), full(g2.shape)],
        out_specs=[row_tile] * 6,
        out_shape=[out] * 6,
        compiler_params=_cparams(("parallel",)),
        name="rwkv_proj",
    )(xf, xf, mix8, w_rkv.astype(BF16), w0.reshape(1, d), w1.astype(BF16), w2.astype(BF16),
      a0.reshape(1, d), a1.astype(BF16), a2.astype(BF16), g1.astype(BF16), g2.astype(BF16))


def _rwkv_scan_kernel(r_ref, k_ref, v_ref, ld_ref, a_ref, kk_ref, ka_ref, rk_ref, lg_ref, lb_ref,
                      o_ref, s_ref, rhat_ref, o0_ref, p_ref, q_ref, dcol_ref, *, chunk, n_chunks, unroll):
    L = chunk

    @pl.when(pl.program_id(1) == 0)
    def _():
        s_ref[...] = jnp.zeros_like(s_ref)

    head0 = lax.broadcasted_iota(jnp.int32, (1, LANES), 1) < HEAD_DIM
    row = lax.broadcasted_iota(jnp.int32, (2 * L, 2 * L), 0)
    col = lax.broadcasted_iota(jnp.int32, (2 * L, 2 * L), 1)
    same_head = (row // L) == (col // L)
    lower_strict = jnp.logical_and(same_head, col < row)
    lower_incl = jnp.logical_and(same_head, col <= row)
    eye = (row == col).astype(F32)
    cum_rows = (lax.broadcasted_iota(jnp.int32, (L, L), 1)
                <= lax.broadcasted_iota(jnp.int32, (L, L), 0)).astype(BF16)
    head_ones = _head_block_ones()
    ones_ln = jnp.ones((L, LANES), BF16)

    def stack(t):
        return jnp.concatenate([jnp.where(head0, t, 0.0), jnp.where(head0, 0.0, t)], axis=0)

    def modified_key(k, a):
        return k * (1.0 + (a - 1.0) * ka_ref[...])

    def prepare(ci):
        sl = pl.ds(pl.multiple_of(ci * L, L), L)
        r, k, v, ld, a = r_ref[sl, :], k_ref[sl, :], v_ref[sl, :], ld_ref[sl, :], a_ref[sl, :]
        kk = k * kk_ref[...]
        kk_sq = _head_sums(kk * kk, head_ones)
        ld3 = jnp.concatenate(_split3(ld), axis=1)
        c3 = _dot(cum_rows, ld3)
        tot3 = _dot_tn(ld3, ones_ln)
        yield
        kk = kk / jnp.maximum(jnp.sqrt(kk_sq), 1e-12)
        k2 = modified_key(k, a)
        a_vec, b_vec = -kk, kk * a
        c = c3[:, :LANES] + c3[:, LANES:2 * LANES] + c3[:, 2 * LANES:]
        dcol_ref[ci] = jnp.exp(tot3[:LANES] + tot3[LANES:2 * LANES] + tot3[2 * LANES:])
        to_end = jnp.exp(c[L - 1:L, :] - c)
        inv = jnp.exp(-c)
        a_st = stack(a_vec * jnp.exp(c - ld))
        r_st = stack(r * jnp.exp(c))
        v_sp = _split2(stack(v))
        bh_sp = _split2(stack(b_vec * to_end))
        kh_sp = _split2(stack(k2 * to_end))
        ar = _split2(jnp.concatenate([a_st, r_st], axis=0))
        bk = _split2(jnp.concatenate([stack(b_vec * inv), stack(k2 * inv)], axis=0))
        prod = _mm3(ar, bk, _dot_nt)
        kv = _mm3(kh_sp, v_sp, _dot_tn)
        yield
        m_ab = jnp.where(lower_strict, prod[:2 * L, :2 * L], 0.0)
        m_ak = jnp.where(lower_strict, prod[:2 * L, 2 * L:], 0.0)
        a_rb = jnp.where(lower_incl, prod[2 * L:, :2 * L], 0.0)
        a_rk = jnp.where(lower_incl, prod[2 * L:, 2 * L:], 0.0)
        w0 = _mm3(_split2(m_ak), v_sp)
        rkv = _mm3(_split2(a_rk), v_sp)
        t_inv = eye
        w = 1
        while w < L:
            below = jnp.logical_and((row // (2 * w)) == (col // (2 * w)),
                                    jnp.logical_and((row % (2 * w)) >= w, (col % (2 * w)) < w))
            t_bf = t_inv.astype(BF16)
            tm = _dot(t_bf, jnp.where(below, m_ab, 0.0).astype(BF16))
            yield
            tmt = _dot(tm.astype(BF16), t_bf)
            yield
            t_inv = t_inv + tmt
            w *= 2
        resid = (eye - t_inv) + _mm3(_split2(m_ab), _split2(t_inv))
        yield
        t_inv = t_inv + _dot(t_inv.astype(BF16), resid.astype(BF16))
        yield
        au = _mm3(_split2(t_inv), _split2(jnp.concatenate([a_st, w0], axis=1)))
        yield
        au_sp = _split2(au)
        ro = _mm3(_split2(a_rb), au_sp)
        pq = _mm3(bh_sp, au_sp, _dot_tn)
        yield
        rhat_ref[ci] = r_st + ro[:, :LANES]
        o0_ref[ci] = ro[:, LANES:] + rkv
        p_ref[ci] = pq[:, :LANES]
        q_ref[ci] = pq[:, LANES:] + kv

    def prepare_group(gi, carry):
        chunks = [prepare(gi * unroll + u) for u in range(unroll)]
        while chunks:
            for g in chunks:
                if next(g, StopIteration) is StopIteration:
                    chunks = []
        return carry

    lax.fori_loop(0, n_chunks // unroll, prepare_group, 0)

    s = s_ref[...]
    for ci in range(n_chunks):
        s_sp = _split2(s)
        o_st = _mm3(_split2(rhat_ref[ci]), s_sp) + o0_ref[ci]
        o_ref[ci * L:(ci + 1) * L, :] = o_st[:L] + o_st[L:]
        s = dcol_ref[ci] * s + _mm3(_split2(p_ref[ci]), s_sp) + q_ref[ci]
    s_ref[...] = s

    o = o_ref[...]
    inv_n = 1.0 / HEAD_DIM
    mu = _head_sums(o, head_ones) * inv_n
    d = o - mu
    var = _head_sums(d * d, head_ones) * inv_n
    r, v = r_ref[...], v_ref[...]
    k2 = modified_key(k_ref[...], a_ref[...])
    bonus = _head_sums(r * k2 * rk_ref[...], head_ones) * v
    o_ref[...] = d * lax.rsqrt(var + GN_EPS) * lg_ref[...] + lb_ref[...] + bonus


def _rwkv_scan(r, k, v, ld, a, seq_len, k_k, k_a, r_k, lnx_g, lnx_b):
    n, d = r.shape
    batch = n // seq_len
    pairs = d // LANES
    rows = min(SCAN_ROWS, seq_len)
    steps = seq_len // rows
    n_chunks = rows // SCAN_CHUNK
    seq = pl.BlockSpec((rows, LANES), lambda p, c: ((p // pairs) * steps + c, p % pairs))
    par = pl.BlockSpec((1, LANES), lambda p, c: (0, p % pairs))
    per_chunk = pltpu.VMEM((n_chunks, 2 * SCAN_CHUNK, LANES), F32)
    return pl.pallas_call(
        functools.partial(_rwkv_scan_kernel, chunk=SCAN_CHUNK, n_chunks=n_chunks,
                          unroll=min(SCAN_UNROLL, n_chunks)),
        grid=(batch * pairs, steps),
        in_specs=[seq] * 5 + [par] * 5,
        out_specs=seq,
        out_shape=jax.ShapeDtypeStruct((n, d), F32),
        scratch_shapes=[pltpu.VMEM((LANES, LANES), F32)] + [per_chunk] * 5,
        compiler_params=_cparams(("parallel", "arbitrary")),
        name="rwkv_scan",
    )(r, k, v, ld, a, k_k.reshape(1, d), k_a.reshape(1, d), r_k.reshape(1, d),
      lnx_g.reshape(1, d), lnx_b.reshape(1, d))


def _proj_ln_kernel(*refs, gated):
    if gated:
        x_ref, o_ref, g_ref, w_ref, lg_ref, lb_ref, out_ref = refs
        h = o_ref[...] * g_ref[...]
    else:
        x_ref, o_ref, w_ref, lg_ref, lb_ref, out_ref = refs
        h = o_ref[...]
    y = ALPHA * x_ref[...] + _dot(h.astype(BF16), w_ref[...])
    out_ref[...] = _layer_norm(y, lg_ref[...], lb_ref[...])


def _proj_ln(xf, o, g, w_o, ln_g, ln_b):
    n, d = xf.shape
    tm = min(ROW_TILE, n)
    row_tile = pl.BlockSpec((tm, d), lambda i: (i, 0))
    full = lambda shape: pl.BlockSpec(shape, lambda i: (0,) * len(shape))
    gated = g is not None
    acts = (xf, o, g) if gated else (xf, o)
    return pl.pallas_call(
        functools.partial(_proj_ln_kernel, gated=gated),
        grid=(n // tm,),
        in_specs=[row_tile] * len(acts) + [full((d, d)), full((1, d)), full((1, d))],
        out_specs=row_tile,
        out_shape=jax.ShapeDtypeStruct((n, d), F32),
        compiler_params=_cparams(("parallel",)),
        name="proj_ln",
    )(*acts, w_o.astype(BF16), ln_g.reshape(1, d), ln_b.reshape(1, d))


def _sb_qkv_kernel(x_ref, w_ref, q_ref, k_ref, v_ref):
    d = x_ref.shape[1]
    qkv = _dot(x_ref[...].astype(BF16), w_ref[...])
    q_ref[...] = (qkv[:, :d] * (HEAD_DIM ** -0.5)).astype(BF16)
    k_ref[...] = qkv[:, d:2 * d].astype(BF16)
    v_ref[...] = qkv[:, 2 * d:].astype(BF16)


def _sb_qkv(xf, w_qkv):
    n, d = xf.shape
    tm = min(ROW_TILE, n)
    row_tile = pl.BlockSpec((tm, d), lambda i: (i, 0))
    out = jax.ShapeDtypeStruct((n, d), BF16)
    return pl.pallas_call(
        _sb_qkv_kernel,
        grid=(n // tm,),
        in_specs=[row_tile, pl.BlockSpec((d, 3 * d), lambda i: (0, 0))],
        out_specs=[row_tile] * 3,
        out_shape=[out] * 3,
        compiler_params=_cparams(("parallel",)),
        name="sb_qkv",
    )(xf, w_qkv.astype(BF16))


def _sb_attn_kernel(q_ref, k_ref, v_ref, o_ref, acc_ref, run_ref, ls_buf, lk_buf, sum_buf, att_buf,
                    *, bq, bk, group):
    i = pl.program_id(1)
    ratio = bq // bk
    m = 2 * bq
    head0 = lax.broadcasted_iota(jnp.int32, (1, LANES), 1) < HEAD_DIM
    q = q_ref[...]
    zero = jnp.zeros_like(q)
    q2 = jnp.concatenate([jnp.where(head0, q, zero), jnp.where(head0, zero, q)], axis=0)
    kr = lax.broadcasted_iota(jnp.int32, (2 * bk, 2 * bk), 0) % bk
    kc = lax.broadcasted_iota(jnp.int32, (2 * bk, 2 * bk), 1)
    suffix = jnp.logical_or(kr > kc, kc >= bk).astype(BF16)
    acc_ref[...] = jnp.zeros_like(acc_ref)
    run_ref[...] = jnp.zeros_like(run_ref)
    row_chunks = [slice(c, c + SB_ROW_CHUNK) for c in range(0, m, SB_ROW_CHUNK)]

    def causal(j, rs):
        q_pos = i * bq + (lax.broadcasted_iota(jnp.int32, (SB_ROW_CHUNK, bk), 0) + rs.start) % bq
        return (j * bk + lax.broadcasted_iota(jnp.int32, (SB_ROW_CHUNK, bk), 1)) < q_pos

    def stages(j, masked, u):
        ks = pl.ds(pl.multiple_of(j * bk, bk), bk)

        def scores():
            ls_buf[u] = _dot_nt(q2, k_ref[ks, :])

        def log_terms():
            for rs in row_chunks:
                z = ls_buf[u, rs, :]
                log_keep = -_softplus(z)
                if masked:
                    log_keep = jnp.where(causal(j, rs), log_keep, 0.0)
                hi, lo = _split2(log_keep)
                lk_buf[u, rs, :bk] = hi
                lk_buf[u, rs, bk:] = lo
                ls_buf[u, rs, :] = z + log_keep

        def suffix_sums():
            sum_buf[u] = _dot(lk_buf[u], suffix)

        def weights():
            for rs in row_chunks:
                run = run_ref[rs, :]
                att = jnp.exp(ls_buf[u, rs, :] + run + sum_buf[u, rs, :bk])
                if masked:
                    att = jnp.where(causal(j, rs), att, 0.0)
                att_buf[u, rs, :] = att.astype(BF16)
                run_ref[rs, :] = run + sum_buf[u, rs, bk:]

        def output():
            acc_ref[...] += _dot(att_buf[u], v_ref[ks, :])

        return (scores, log_terms, suffix_sums, weights, output)

    def run_tiles(tiles, masked):
        per_tile = [stages(j, masked, u) for u, j in enumerate(tiles)]
        for stage in zip(*per_tile):
            for emit in stage:
                emit()

    run_tiles([(i + 1) * ratio - 1 - u for u in range(ratio)], True)
    n_off = i * ratio

    def live(step):
        return jnp.logical_and(step < n_off // group, jnp.max(run_ref[...]) > SB_EXP_ZERO)

    def body(step):
        run_tiles([n_off - 1 - step * group - u for u in range(group)], False)
        return step + 1

    lax.while_loop(live, body, 0)
    o_ref[...] = jnp.where(head0, acc_ref[:bq, :], acc_ref[bq:, :])


def _sb_attn(q, k, v, seq_len):
    n, d = q.shape
    batch = n // seq_len
    pairs = d // LANES
    bq = min(SB_Q_BLOCK, seq_len)
    bk = SB_K_BLOCK
    nq = seq_len // bq
    ratio = bq // bk
    group = min(SB_GROUP, ratio)
    assert ratio % group == 0
    in_flight = ratio
    q_spec = pl.BlockSpec((bq, LANES), lambda p, i: ((p // pairs) * nq + i, p % pairs))
    kv_spec = pl.BlockSpec((seq_len, LANES), lambda p, i: (p // pairs, p % pairs))
    return pl.pallas_call(
        functools.partial(_sb_attn_kernel, bq=bq, bk=bk, group=group),
        grid=(batch * pairs, nq),
        in_specs=[q_spec, kv_spec, kv_spec],
        out_specs=q_spec,
        out_shape=jax.ShapeDtypeStruct((n, d), F32),
        scratch_shapes=[pltpu.VMEM((2 * bq, LANES), F32), pltpu.VMEM((2 * bq, LANES), F32),
                        pltpu.VMEM((in_flight, 2 * bq, bk), F32), pltpu.VMEM((in_flight, 2 * bq, 2 * bk), BF16),
                        pltpu.VMEM((in_flight, 2 * bq, 2 * bk), F32), pltpu.VMEM((in_flight, 2 * bq, bk), BF16)],
        compiler_params=_cparams(("parallel", "arbitrary")),
        name="sb_attn",
    )(q, k, v)


def _router_kernel(x_ref, w_ref, b_ref, idx_ref, gate_ref):
    logits = _dot_nt(w_ref[...], x_ref[...], HIGHEST) + b_ref[...]
    n_exp = logits.shape[0]
    e_idx = lax.broadcasted_iota(jnp.int32, logits.shape, 0)
    vals, idxs = [], []
    work = logits
    for _ in range(TOP_K):
        m = jnp.max(work, axis=0, keepdims=True)
        am = jnp.min(jnp.where(work == m, e_idx, n_exp), axis=0, keepdims=True)
        vals.append(m)
        idxs.append(am)
        work = jnp.where(e_idx == am, -jnp.inf, work)
    ex = [jnp.exp(vj - vals[0]) for vj in vals]
    inv = 1.0 / (ex[0] + ex[1] + ex[2] + ex[3])
    pad_i = jnp.zeros((8 - TOP_K, logits.shape[1]), jnp.int32)
    pad_f = jnp.zeros((8 - TOP_K, logits.shape[1]), F32)
    idx_ref[...] = jnp.concatenate(idxs + [pad_i], axis=0)
    gate_ref[...] = jnp.concatenate([e * inv for e in ex] + [pad_f], axis=0)


def _router(xf, w_router, b_router):
    n, d = xf.shape
    n_exp = w_router.shape[1]
    tm = min(ROW_TILE, n)
    out_spec = pl.BlockSpec((8, tm), lambda i: (0, i))
    idx, gate = pl.pallas_call(
        _router_kernel,
        grid=(n // tm,),
        in_specs=[pl.BlockSpec((tm, d), lambda i: (i, 0)), pl.BlockSpec((n_exp, d), lambda i: (0, 0)),
                  pl.BlockSpec((n_exp, 1), lambda i: (0, 0))],
        out_specs=[out_spec, out_spec],
        out_shape=[jax.ShapeDtypeStruct((8, n), jnp.int32), jax.ShapeDtypeStruct((8, n), F32)],
        compiler_params=_cparams(("parallel",)),
        name="moe_router",
    )(xf, w_router.T, b_router.reshape(n_exp, 1))
    return idx[:TOP_K].T, gate[:TOP_K].T


def _group_rows(top_idx, n_exp, rows):
    n = top_idx.shape[0]
    n_assign = n * TOP_K
    expert = top_idx.reshape(-1)
    order = jnp.argsort(expert, stable=True).astype(jnp.int32)
    counts = jnp.sum(expert[:, None] == jnp.arange(n_exp, dtype=jnp.int32)[None, :], axis=0, dtype=jnp.int32)
    padded = (counts + rows - 1) // rows * rows
    pad_end = jnp.cumsum(padded)
    pad_start = pad_end - padded
    n_blocks = n_assign // rows + n_exp
    block_pos = jnp.arange(n_blocks, dtype=jnp.int32)[:, None] * rows
    block_expert = jnp.minimum(jnp.sum(block_pos >= pad_end[None, :], axis=1, dtype=jnp.int32), n_exp - 1)
    pos = jnp.arange(n_blocks * rows, dtype=jnp.int32)
    blk = pos // rows
    past = pos[:, None] >= pad_end[None, :]
    n_past = jnp.sum(past, axis=1, dtype=jnp.int32)
    gap = jnp.sum(jnp.where(past, (padded - counts)[None, :], 0), axis=1, dtype=jnp.int32)
    n_real_end = jnp.sum(pos[:, None] >= (pad_start + counts)[None, :], axis=1, dtype=jnp.int32)
    valid = jnp.logical_and(n_real_end == n_past, pos < pad_end[-1])
    assign = order[jnp.clip(pos - gap, 0, n_assign - 1)]
    tok = assign // TOP_K
    slot = assign % TOP_K
    src = jnp.where(valid, tok, 0).reshape(n_blocks, 1, rows)
    dump = n_assign + (blk % 2) * rows + pos % rows
    dst = jnp.where(valid, slot * n + tok, dump).reshape(n_blocks, 1, rows)
    before_first = (n_assign + rows + jnp.arange(rows, dtype=jnp.int32)).reshape(1, 1, rows)
    src = jnp.concatenate([src, jnp.zeros((1, 1, rows), jnp.int32)], axis=0)
    dst_prev = jnp.concatenate([before_first, dst], axis=0)
    block_expert = jnp.concatenate([block_expert, block_expert[-1:]])
    n_used = (pad_end[-1] // rows).astype(jnp.int32).reshape(1)
    return src, dst_prev, block_expert, n_used


def _swiglu_split_kernel(w_ref, even_ref, odd_ref, glu_ref, lin_ref):
    w = w_ref[0].astype(BF16)
    glu_ref[0] = _dot(w, even_ref[...]).astype(BF16)
    lin_ref[0] = _dot(w, odd_ref[...]).astype(BF16)


def _swiglu_split(w1):
    n_exp, d, f2 = w1.shape
    cols = min(SPLIT_COLS, f2)
    r = jnp.arange(cols, dtype=jnp.int32)[:, None]
    c = jnp.arange(cols // 2, dtype=jnp.int32)[None, :]
    even = (r == 2 * c).astype(BF16)
    odd = (r == 2 * c + 1).astype(BF16)
    sel = pl.BlockSpec((cols, cols // 2), lambda e, j: (0, 0))
    out_spec = pl.BlockSpec((1, d, cols // 2), lambda e, j: (e, 0, j))
    out = jax.ShapeDtypeStruct((n_exp, d, f2 // 2), BF16)
    return pl.pallas_call(
        _swiglu_split_kernel,
        grid=(n_exp, f2 // cols),
        in_specs=[pl.BlockSpec((1, d, cols), lambda e, j: (e, 0, j)), sel, sel],
        out_specs=[out_spec, out_spec],
        out_shape=[out, out],
        compiler_params=_cparams(("parallel", "parallel")),
        name="swiglu_split",
    )(w1, even, odd)


def _expert_kernel(bexp_ref, nused_ref, src0_ref, src_next_ref, dst_prev_ref, x_hbm, w1g_ref, w1l_ref,
                   b1g_ref, b1l_ref, w2_ref, b2_ref, y_hbm, xbuf, ybuf, gsem, ssem, isem, *, rows, n_tok_rows):
    del bexp_ref
    b = pl.program_id(0)
    n_used = nused_ref[0]
    slot = b % 2
    other = 1 - slot

    def gather_start(idx_ref, s, i, priority=0):
        pltpu.make_async_copy(x_hbm.at[pl.ds(idx_ref[0, 0, i], 1), :], xbuf.at[s, pl.ds(i, 1), :],
                              gsem.at[s]).start(priority=priority)

    def scatter_start(s, i, priority=0):
        pltpu.make_async_copy(ybuf.at[s, pl.ds(i, 1), :], y_hbm.at[pl.ds(dst_prev_ref[0, 0, i], 1), :],
                              ssem.at[s]).start(priority=priority)

    def wait_gather(s):
        pltpu.make_async_copy(x_hbm.at[pl.ds(0, rows), :], xbuf.at[s], gsem.at[s]).wait()

    def wait_scatter(s):
        pltpu.make_async_copy(ybuf.at[s], y_hbm.at[pl.ds(0, rows), :], ssem.at[s]).wait()

    @pl.when(b == 0)
    def _():
        ybuf[...] = jnp.zeros_like(ybuf)
        for half in range(2):
            fill = pltpu.make_async_copy(ybuf.at[half], y_hbm.at[pl.ds(n_tok_rows + half * rows, rows), :],
                                         isem.at[0])
            fill.start()
            fill.wait()

        def first(i, carry):
            gather_start(src0_ref, 0, i)
            return carry
        lax.fori_loop(0, rows, first, 0, unroll=DMA_UNROLL)

    @pl.when(jnp.logical_and(b >= 1, b <= n_used))
    def _():
        wait_scatter(slot)

    @pl.when(b <= n_used)
    def _():
        wait_gather(slot)

    @pl.when(b < n_used)
    def _():
        per_piece = rows // DMA_PIECES

        def copies(piece):
            for i in range(piece * per_piece, (piece + 1) * per_piece):
                gather_start(src_next_ref, other, i, priority=i % 2)
                scatter_start(other, i, priority=i % 2)

        xb = xbuf[slot].astype(BF16)
        copies(0)
        glu = jnp.minimum(_dot(xb, w1g_ref[0]) + b1g_ref[0], SWIGLU_LIMIT)
        copies(1)
        lin = jnp.clip(_dot(xb, w1l_ref[0]) + b1l_ref[0], -SWIGLU_LIMIT, SWIGLU_LIMIT)
        copies(2)
        act = glu * _sigmoid(SWIGLU_ALPHA * glu) * (lin + 1.0)
        copies(3)
        ybuf[slot] = _dot(act.astype(BF16), w2_ref[0]) + b2_ref[0]

    @pl.when(b == n_used)
    def _():
        def last(i, carry):
            scatter_start(other, i)
            return carry
        lax.fori_loop(0, rows, last, 0, unroll=DMA_UNROLL)
        wait_scatter(other)


def _moe_experts(xf, src, dst_prev, block_expert, n_used, w1g, w1l, b1g, b1l, w2, b2):
    n, d = xf.shape
    n_steps, _, rows = src.shape
    n_exp, _, f = w1g.shape
    idx_spec = lambda index_map: pl.BlockSpec((1, 1, rows), index_map, memory_space=pltpu.SMEM)
    per_expert = lambda shape: pl.BlockSpec((1,) + shape, lambda b, be, nu: (be[b], 0, 0))
    grid_spec = pltpu.PrefetchScalarGridSpec(
        num_scalar_prefetch=2,
        grid=(n_steps,),
        in_specs=[idx_spec(lambda b, be, nu: (0, 0, 0)),
                  idx_spec(lambda b, be, nu: (jnp.minimum(b + 1, n_steps - 1), 0, 0)),
                  idx_spec(lambda b, be, nu: (b, 0, 0)), pl.BlockSpec(memory_space=pl.ANY),
                  per_expert((d, f)), per_expert((d, f)), per_expert((1, f)), per_expert((1, f)),
                  per_expert((f, d)), per_expert((1, d))],
        out_specs=pl.BlockSpec(memory_space=pl.ANY),
        scratch_shapes=[pltpu.VMEM((2, rows, d), F32), pltpu.VMEM((2, rows, d), F32),
                        pltpu.SemaphoreType.DMA((2,)), pltpu.SemaphoreType.DMA((2,)),
                        pltpu.SemaphoreType.DMA((1,))],
    )
    return pl.pallas_call(
        functools.partial(_expert_kernel, rows=rows, n_tok_rows=TOP_K * n),
        grid_spec=grid_spec,
        out_shape=jax.ShapeDtypeStruct((TOP_K * n + 2 * rows, d), F32),
        compiler_params=_cparams(("arbitrary",)),
        name="moe_experts",
    )(block_expert, n_used, src, src, dst_prev, xf, w1g, w1l, b1g, b1l, w2, b2)


def _combine_kernel(x_ref, gate_ref, y0_ref, y1_ref, y2_ref, y3_ref, lg_ref, lb_ref, out_ref):
    gate = gate_ref[...]
    f = (gate[:, 0:1] * y0_ref[...] + gate[:, 1:2] * y1_ref[...]
         + gate[:, 2:3] * y2_ref[...] + gate[:, 3:4] * y3_ref[...])
    out_ref[...] = _layer_norm(ALPHA * x_ref[...] + f, lg_ref[...], lb_ref[...])


def _moe_combine(xf, gate, y, ln_g, ln_b):
    n, d = xf.shape
    tm = min(ROW_TILE, n)
    tiles = n // tm
    row_tile = pl.BlockSpec((tm, d), lambda i: (i, 0))
    slot_tile = lambda j: pl.BlockSpec((tm, d), lambda i: (j * tiles + i, 0))
    vec = pl.BlockSpec((1, d), lambda i: (0, 0))
    return pl.pallas_call(
        _combine_kernel,
        grid=(tiles,),
        in_specs=[row_tile, pl.BlockSpec((tm, TOP_K), lambda i: (i, 0))] + [slot_tile(j) for j in range(TOP_K)]
                 + [vec, vec],
        out_specs=row_tile,
        out_shape=jax.ShapeDtypeStruct((n, d), F32),
        compiler_params=_cparams(("parallel",)),
        name="moe_combine",
    )(xf, gate, y, y, y, y, ln_g.reshape(1, d), ln_b.reshape(1, d))


def _moe(xf, w_router, b_router, w1g, w1l, b1, w2, b2, ln_g, ln_b):
    n, d = xf.shape
    n_exp = w_router.shape[1]
    rows = min(EXPERT_ROWS, n)
    top_idx, gate = _router(xf, w_router, b_router)
    src, dst_prev, block_expert, n_used = _group_rows(top_idx, n_exp, rows)
    y = _moe_experts(xf, src, dst_prev, block_expert, n_used, w1g, w1l, b1[:, None, 0::2], b1[:, None, 1::2],
                     w2.astype(BF16), b2[:, None, :])
    return _moe_combine(xf, gate, y, ln_g, ln_b)


def kernel(x, rw_mix, rw_w_rkv, rw_w0, rw_w1, rw_w2, rw_a0, rw_a1, rw_a2, rw_g1, rw_g2, rw_k_k, rw_k_a, rw_r_k,
           rw_lnx_g, rw_lnx_b, rw_w_o, sb_w_qkv, sb_w_o, moe_w_router, moe_b_router, moe_w1, moe_b1, moe_w2,
           moe_b2, ln_g, ln_b):
    batch, seq_len, d = x.shape
    xf = x.reshape(batch * seq_len, d)
    n_layers, n_exp = moe_w1.shape[:2]
    w1g, w1l = _swiglu_split(moe_w1.reshape((n_layers * n_exp,) + moe_w1.shape[2:]))
    w1g = w1g.reshape((n_layers, n_exp) + w1g.shape[1:])
    w1l = w1l.reshape((n_layers, n_exp) + w1l.shape[1:])
    for i in range(DEPTH):
        j = i // 2
        if i % 2 == 0:
            r, k, v, ld, a, g = _rwkv_proj(xf, seq_len, rw_mix[j], rw_w_rkv[j], rw_w0[j], rw_w1[j], rw_w2[j],
                                           rw_a0[j], rw_a1[j], rw_a2[j], rw_g1[j], rw_g2[j])
            o = _rwkv_scan(r, k, v, ld, a, seq_len, rw_k_k[j], rw_k_a[j], rw_r_k[j], rw_lnx_g[j], rw_lnx_b[j])
            xf = _proj_ln(xf, o, g, rw_w_o[j], ln_g[i, 0], ln_b[i, 0])
        else:
            q, k, v = _sb_qkv(xf, sb_w_qkv[j])
            o = _sb_attn(q, k, v, seq_len)
            xf = _proj_ln(xf, o, None, sb_w_o[j], ln_g[i, 0], ln_b[i, 0])
        xf = _moe(xf, moe_w_router[i], moe_b_router[i], w1g[i], w1l[i], moe_b1[i], moe_w2[i], moe_b2[i],
                  ln_g[i, 1], ln_b[i, 1])
    return xf.reshape(batch, seq_len, d)
```

```python
import functools

import jax
import jax.numpy as jnp
from jax import lax
from jax.experimental import pallas as pl
from jax.experimental.pallas import tpu as pltpu

F32 = jnp.float32
BF16 = jnp.bfloat16
HIGHEST = lax.Precision.HIGHEST

DEPTH = 2
ALPHA = (2 * DEPTH) ** 0.25
LN_EPS = 1e-5
GN_EPS = 64e-5
HEAD_DIM = 64
LANES = 128
TOP_K = 4
SWIGLU_LIMIT = 7.0
SWIGLU_ALPHA = 1.702

ROW_TILE = 256
SCAN_CHUNK = 64
SCAN_ROWS = 1024
SCAN_UNROLL = 16
SB_Q_BLOCK = 256
SB_K_BLOCK = LANES
SB_GROUP = 2
SB_EXP_ZERO = -104.0
SB_ROW_CHUNK = 64
EXPERT_ROWS = 256
DMA_UNROLL = 8
DMA_PIECES = 4
SPLIT_COLS = 512
VMEM_LIMIT = 56 * 1024 * 1024


def _cparams(semantics):
    return pltpu.CompilerParams(dimension_semantics=semantics, vmem_limit_bytes=VMEM_LIMIT)


def _dot(a, b, precision=None):
    return jnp.dot(a, b, preferred_element_type=F32, precision=precision)


def _dot_nt(a, b, precision=None):
    return lax.dot_general(a, b, (((1,), (1,)), ((), ())), preferred_element_type=F32, precision=precision)


def _dot_tn(a, b, precision=None):
    return lax.dot_general(a, b, (((0,), (0,)), ((), ())), preferred_element_type=F32, precision=precision)


def _split2(a):
    hi = a.astype(BF16)
    return hi, (a - hi.astype(F32)).astype(BF16)


def _split3(a):
    hi = a.astype(BF16)
    rest = a - hi.astype(F32)
    mid = rest.astype(BF16)
    return hi, mid, (rest - mid.astype(F32)).astype(BF16)


def _mm3(a, b, dot=_dot):
    a_axis = 0 if dot is _dot_tn else 1
    b_axis = 1 if dot is _dot_nt else 0
    return dot(jnp.concatenate([a[0], a[0], a[1]], axis=a_axis), jnp.concatenate([b[0], b[1], b[0]], axis=b_axis))


def _softplus(u):
    return jnp.maximum(u, 0.0) + jnp.log(1.0 + jnp.exp(-jnp.abs(u)))


def _sigmoid(u):
    return 1.0 / (1.0 + jnp.exp(-u))


def _layer_norm(y, g, b):
    mu = jnp.mean(y, axis=-1, keepdims=True)
    d = y - mu
    var = jnp.mean(d * d, axis=-1, keepdims=True)
    return d * lax.rsqrt(var + LN_EPS) * g + b


def _head_block_ones():
    r = lax.broadcasted_iota(jnp.int32, (LANES, LANES), 0) // HEAD_DIM
    c = lax.broadcasted_iota(jnp.int32, (LANES, LANES), 1) // HEAD_DIM
    return (r == c).astype(BF16)


def _head_sums(x, head_ones):
    m = x.shape[0]
    s = _dot(jnp.concatenate(_split3(x), axis=0), head_ones)
    return s[:m] + s[m:2 * m] + s[2 * m:]


def _rwkv_proj_kernel(x_ref, xp_ref, mix_ref, wrkv_ref, w0_ref, w1_ref, w2_ref, a0_ref, a1_ref, a2_ref,
                      g1_ref, g2_ref, r_ref, k_ref, v_ref, ld_ref, a_ref, g_ref, *, tiles_per_seq):
    i = pl.program_id(0)
    x = x_ref[...]
    prev_last = jnp.where(i % tiles_per_seq == 0, 0.0, xp_ref[7:8, :])
    row = lax.broadcasted_iota(jnp.int32, x.shape, 0)
    x_prev = jnp.where(row == 0, prev_last, pltpu.roll(x, 1, 0))
    xx = x_prev - x

    def mixed(j):
        return (x + xx * mix_ref[j:j + 1, :]).astype(BF16)

    r_ref[...] = _dot(mixed(0), wrkv_ref[0])
    w = w0_ref[...] + _dot(jnp.tanh(_dot(mixed(1), w1_ref[...])).astype(BF16), w2_ref[...])
    ld_ref[...] = -jnp.exp(-_softplus(-w) - 0.5)
    k_ref[...] = _dot(mixed(2), wrkv_ref[1])
    v_ref[...] = _dot(mixed(3), wrkv_ref[2])
    a_ref[...] = _sigmoid(a0_ref[...] + _dot(_dot(mixed(4), a1_ref[...]).astype(BF16), a2_ref[...]))
    g_ref[...] = _dot(_sigmoid(_dot(mixed(5), g1_ref[...])).astype(BF16), g2_ref[...])


def _rwkv_proj(xf, seq_len, mix, w_rkv, w0, w1, w2, a0, a1, a2, g1, g2):
    n, d = xf.shape
    tm = min(ROW_TILE, seq_len)
    full = lambda shape: pl.BlockSpec(shape, lambda i: (0,) * len(shape))
    row_tile = pl.BlockSpec((tm, d), lambda i: (i, 0))
    prev_rows = pl.BlockSpec((8, d), lambda i: (jnp.maximum(i * (tm // 8) - 1, 0), 0))
    mix8 = jnp.concatenate([mix, jnp.zeros((2, d), F32)], axis=0)
    out = jax.ShapeDtypeStruct((n, d), F32)
    return pl.pallas_call(
        functools.partial(_rwkv_proj_kernel, tiles_per_seq=seq_len // tm),
        grid=(n // tm,),
        in_specs=[row_tile, prev_rows, full((8, d)), full(w_rkv.shape), full((1, d)), full(w1.shape),
                  full(w2.shape), full((1, d)), full(a1.shape), full(a2.shape), full(g1.shape), full(g2.shape)],
        out_specs=[row_tile] * 6,
        out_shape=[out] * 6,
        compiler_params=_cparams(("parallel",)),
        name="rwkv_proj",
    )(xf, xf, mix8, w_rkv.astype(BF16), w0.reshape(1, d), w1.astype(BF16), w2.astype(BF16),
      a0.reshape(1, d), a1.astype(BF16), a2.astype(BF16), g1.astype(BF16), g2.astype(BF16))


def _rwkv_scan_kernel(r_ref, k_ref, v_ref, ld_ref, a_ref, kk_ref, ka_ref, rk_ref, lg_ref, lb_ref,
                      o_ref, s_ref, rhat_ref, o0_ref, p_ref, q_ref, dcol_ref, *, chunk, n_chunks, unroll):
    L = chunk

    @pl.when(pl.program_id(1) == 0)
    def _():
        s_ref[...] = jnp.zeros_like(s_ref)

    head0 = lax.broadcasted_iota(jnp.int32, (1, LANES), 1) < HEAD_DIM
    row = lax.broadcasted_iota(jnp.int32, (2 * L, 2 * L), 0)
    col = lax.broadcasted_iota(jnp.int32, (2 * L, 2 * L), 1)
    same_head = (row // L) == (col // L)
    lower_strict = jnp.logical_and(same_head, col < row)
    lower_incl = jnp.logical_and(same_head, col <= row)
    eye = (row == col).astype(F32)
    cum_rows = (lax.broadcasted_iota(jnp.int32, (L, L), 1)
                <= lax.broadcasted_iota(jnp.int32, (L, L), 0)).astype(BF16)
    head_ones = _head_block_ones()
    ones_ln = jnp.ones((L, LANES), BF16)

    def stack(t):
        return jnp.concatenate([jnp.where(head0, t, 0.0), jnp.where(head0, 0.0, t)], axis=0)

    def modified_key(k, a):
        return k * (1.0 + (a - 1.0) * ka_ref[...])

    def prepare(ci):
        sl = pl.ds(pl.multiple_of(ci * L, L), L)
        r, k, v, ld, a = r_ref[sl, :], k_ref[sl, :], v_ref[sl, :], ld_ref[sl, :], a_ref[sl, :]
        kk = k * kk_ref[...]
        kk_sq = _head_sums(kk * kk, head_ones)
        ld3 = jnp.concatenate(_split3(ld), axis=1)
        c3 = _dot(cum_rows, ld3)
        tot3 = _dot_tn(ld3, ones_ln)
        yield
        kk = kk / jnp.maximum(jnp.sqrt(kk_sq), 1e-12)
        k2 = modified_key(k, a)
        a_vec, b_vec = -kk, kk * a
        c = c3[:, :LANES] + c3[:, LANES:2 * LANES] + c3[:, 2 * LANES:]
        dcol_ref[ci] = jnp.exp(tot3[:LANES] + tot3[LANES:2 * LANES] + tot3[2 * LANES:])
        to_end = jnp.exp(c[L - 1:L, :] - c)
        inv = jnp.exp(-c)
        a_st = stack(a_vec * jnp.exp(c - ld))
        r_st = stack(r * jnp.exp(c))
        v_sp = _split2(stack(v))
        bh_sp = _split2(stack(b_vec * to_end))
        kh_sp = _split2(stack(k2 * to_end))
        ar = _split2(jnp.concatenate([a_st, r_st], axis=0))
        bk = _split2(jnp.concatenate([stack(b_vec * inv), stack(k2 * inv)], axis=0))
        prod = _mm3(ar, bk, _dot_nt)
        kv = _mm3(kh_sp, v_sp, _dot_tn)
        yield
        m_ab = jnp.where(lower_strict, prod[:2 * L, :2 * L], 0.0)
        m_ak = jnp.where(lower_strict, prod[:2 * L, 2 * L:], 0.0)
        a_rb = jnp.where(lower_incl, prod[2 * L:, :2 * L], 0.0)
        a_rk = jnp.where(lower_incl, prod[2 * L:, 2 * L:], 0.0)
        w0 = _mm3(_split2(m_ak), v_sp)
        rkv = _mm3(_split2(a_rk), v_sp)
        t_inv = eye
        w = 1
        while w < L:
            below = jnp.logical_and((row // (2 * w)) == (col // (2 * w)),
                                    jnp.logical_and((row % (2 * w)) >= w, (col % (2 * w)) < w))
            t_bf = t_inv.astype(BF16)
            tm = _dot(t_bf, jnp.where(below, m_ab, 0.0).astype(BF16))
            yield
            tmt = _dot(tm.astype(BF16), t_bf)
            yield
            t_inv = t_inv + tmt
            w *= 2
        resid = (eye - t_inv) + _mm3(_split2(m_ab), _split2(t_inv))
        yield
        t_inv = t_inv + _dot(t_inv.astype(BF16), resid.astype(BF16))
        yield
        au = _mm3(_split2(t_inv), _split2(jnp.concatenate([a_st, w0], axis=1)))
        yield
        au_sp = _split2(au)
        ro = _mm3(_split2(a_rb), au_sp)
        pq = _mm3(bh_sp, au_sp, _dot_tn)
        yield
        rhat_ref[ci] = r_st + ro[:, :LANES]
        o0_ref[ci] = ro[:, LANES:] + rkv
        p_ref[ci] = pq[:, :LANES]
        q_ref[ci] = pq[:, LANES:] + kv

    def prepare_group(gi, carry):
        chunks = [prepare(gi * unroll + u) for u in range(unroll)]
        while chunks:
            for g in chunks:
                if next(g, StopIteration) is StopIteration:
                    chunks = []
        return carry

    lax.fori_loop(0, n_chunks // unroll, prepare_group, 0)

    s = s_ref[...]
    for ci in range(n_chunks):
        s_sp = _split2(s)
        o_st = _mm3(_split2(rhat_ref[ci]), s_sp) + o0_ref[ci]
        o_ref[ci * L:(ci + 1) * L, :] = o_st[:L] + o_st[L:]
        s = dcol_ref[ci] * s + _mm3(_split2(p_ref[ci]), s_sp) + q_ref[ci]
    s_ref[...] = s

    o = o_ref[...]
    inv_n = 1.0 / HEAD_DIM
    mu = _head_sums(o, head_ones) * inv_n
    d = o - mu
    var = _head_sums(d * d, head_ones) * inv_n
    r, v = r_ref[...], v_ref[...]
    k2 = modified_key(k_ref[...], a_ref[...])
    bonus = _head_sums(r * k2 * rk_ref[...], head_ones) * v
    o_ref[...] = d * lax.rsqrt(var + GN_EPS) * lg_ref[...] + lb_ref[...] + bonus


def _rwkv_scan(r, k, v, ld, a, seq_len, k_k, k_a, r_k, lnx_g, lnx_b):
    n, d = r.shape
    batch = n // seq_len
    pairs = d // LANES
    rows = min(SCAN_ROWS, seq_len)
    steps = seq_len // rows
    n_chunks = rows // SCAN_CHUNK
    seq = pl.BlockSpec((rows, LANES), lambda p, c: ((p // pairs) * steps + c, p % pairs))
    par = pl.BlockSpec((1, LANES), lambda p, c: (0, p % pairs))
    per_chunk = pltpu.VMEM((n_chunks, 2 * SCAN_CHUNK, LANES), F32)
    return pl.pallas_call(
        functools.partial(_rwkv_scan_kernel, chunk=SCAN_CHUNK, n_chunks=n_chunks,
                          unroll=min(SCAN_UNROLL, n_chunks)),
        grid=(batch * pairs, steps),
        in_specs=[seq] * 5 + [par] * 5,
        out_specs=seq,
        out_shape=jax.ShapeDtypeStruct((n, d), F32),
        scratch_shapes=[pltpu.VMEM((LANES, LANES), F32)] + [per_chunk] * 5,
        compiler_params=_cparams(("parallel", "arbitrary")),
        name="rwkv_scan",
    )(r, k, v, ld, a, k_k.reshape(1, d), k_a.reshape(1, d), r_k.reshape(1, d),
      lnx_g.reshape(1, d), lnx_b.reshape(1, d))


def _proj_ln_kernel(*refs, gated):
    if gated:
        x_ref, o_ref, g_ref, w_ref, lg_ref, lb_ref, out_ref = refs
        h = o_ref[...] * g_ref[...]
    else:
        x_ref, o_ref, w_ref, lg_ref, lb_ref, out_ref = refs
        h = o_ref[...]
    y = ALPHA * x_ref[...] + _dot(h.astype(BF16), w_ref[...])
    out_ref[...] = _layer_norm(y, lg_ref[...], lb_ref[...])


def _proj_ln(xf, o, g, w_o, ln_g, ln_b):
    n, d = xf.shape
    tm = min(ROW_TILE, n)
    row_tile = pl.BlockSpec((tm, d), lambda i: (i, 0))
    full = lambda shape: pl.BlockSpec(shape, lambda i: (0,) * len(shape))
    gated = g is not None
    acts = (xf, o, g) if gated else (xf, o)
    return pl.pallas_call(
        functools.partial(_proj_ln_kernel, gated=gated),
        grid=(n // tm,),
        in_specs=[row_tile] * len(acts) + [full((d, d)), full((1, d)), full((1, d))],
        out_specs=row_tile,
        out_shape=jax.ShapeDtypeStruct((n, d), F32),
        compiler_params=_cparams(("parallel",)),
        name="proj_ln",
    )(*acts, w_o.astype(BF16), ln_g.reshape(1, d), ln_b.reshape(1, d))


def _sb_qkv_kernel(x_ref, w_ref, q_ref, k_ref, v_ref):
    d = x_ref.shape[1]
    qkv = _dot(x_ref[...].astype(BF16), w_ref[...])
    q_ref[...] = (qkv[:, :d] * (HEAD_DIM ** -0.5)).astype(BF16)
    k_ref[...] = qkv[:, d:2 * d].astype(BF16)
    v_ref[...] = qkv[:, 2 * d:].astype(BF16)


def _sb_qkv(xf, w_qkv):
    n, d = xf.shape
    tm = min(ROW_TILE, n)
    row_tile = pl.BlockSpec((tm, d), lambda i: (i, 0))
    out = jax.ShapeDtypeStruct((n, d), BF16)
    return pl.pallas_call(
        _sb_qkv_kernel,
        grid=(n // tm,),
        in_specs=[row_tile, pl.BlockSpec((d, 3 * d), lambda i: (0, 0))],
        out_specs=[row_tile] * 3,
        out_shape=[out] * 3,
        compiler_params=_cparams(("parallel",)),
        name="sb_qkv",
    )(xf, w_qkv.astype(BF16))


def _sb_attn_kernel(q_ref, k_ref, v_ref, o_ref, acc_ref, run_ref, ls_buf, lk_buf, sum_buf, att_buf,
                    *, bq, bk, group):
    i = pl.program_id(1)
    ratio = bq // bk
    m = 2 * bq
    head0 = lax.broadcasted_iota(jnp.int32, (1, LANES), 1) < HEAD_DIM
    q = q_ref[...]
    zero = jnp.zeros_like(q)
    q2 = jnp.concatenate([jnp.where(head0, q, zero), jnp.where(head0, zero, q)], axis=0)
    kr = lax.broadcasted_iota(jnp.int32, (2 * bk, 2 * bk), 0) % bk
    kc = lax.broadcasted_iota(jnp.int32, (2 * bk, 2 * bk), 1)
    suffix = jnp.logical_or(kr > kc, kc >= bk).astype(BF16)
    acc_ref[...] = jnp.zeros_like(acc_ref)
    run_ref[...] = jnp.zeros_like(run_ref)
    row_chunks = [slice(c, c + SB_ROW_CHUNK) for c in range(0, m, SB_ROW_CHUNK)]

    def causal(j, rs):
        q_pos = i * bq + (lax.broadcasted_iota(jnp.int32, (SB_ROW_CHUNK, bk), 0) + rs.start) % bq
        return (j * bk + lax.broadcasted_iota(jnp.int32, (SB_ROW_CHUNK, bk), 1)) < q_pos

    def stages(j, masked, u):
        ks = pl.ds(pl.multiple_of(j * bk, bk), bk)

        def scores():
            ls_buf[u] = _dot_nt(q2, k_ref[ks, :])

        def log_terms():
            for rs in row_chunks:
                z = ls_buf[u, rs, :]
                log_keep = -_softplus(z)
                if masked:
                    log_keep = jnp.where(causal(j, rs), log_keep, 0.0)
                hi, lo = _split2(log_keep)
                lk_buf[u, rs, :bk] = hi
                lk_buf[u, rs, bk:] = lo
                ls_buf[u, rs, :] = z + log_keep

        def suffix_sums():
            sum_buf[u] = _dot(lk_buf[u], suffix)

        def weights():
            for rs in row_chunks:
                run = run_ref[rs, :]
                att = jnp.exp(ls_buf[u, rs, :] + run + sum_buf[u, rs, :bk])
                if masked:
                    att = jnp.where(causal(j, rs), att, 0.0)
                att_buf[u, rs, :] = att.astype(BF16)
                run_ref[rs, :] = run + sum_buf[u, rs, bk:]

        def output():
            acc_ref[...] += _dot(att_buf[u], v_ref[ks, :])

        return (scores, log_terms, suffix_sums, weights, output)

    def run_tiles(tiles, masked):
        per_tile = [stages(j, masked, u) for u, j in enumerate(tiles)]
        for stage in zip(*per_tile):
            for emit in stage:
                emit()

    run_tiles([(i + 1) * ratio - 1 - u for u in range(ratio)], True)
    n_off = i * ratio

    def live(step):
        return jnp.logical_and(step < n_off // group, jnp.max(run_ref[...]) > SB_EXP_ZERO)

    def body(step):
        run_tiles([n_off - 1 - step * group - u for u in range(group)], False)
        return step + 1

    lax.while_loop(live, body, 0)
    o_ref[...] = jnp.where(head0, acc_ref[:bq, :], acc_ref[bq:, :])


def _sb_attn(q, k, v, seq_len):
    n, d = q.shape
    batch = n // seq_len
    pairs = d // LANES
    bq = min(SB_Q_BLOCK, seq_len)
    bk = SB_K_BLOCK
    nq = seq_len // bq
    ratio = bq // bk
    group = min(SB_GROUP, ratio)
    assert ratio % group == 0
    in_flight = ratio
    q_spec = pl.BlockSpec((bq, LANES), lambda p, i: ((p // pairs) * nq + i, p % pairs))
    kv_spec = pl.BlockSpec((seq_len, LANES), lambda p, i: (p // pairs, p % pairs))
    return pl.pallas_call(
        functools.partial(_sb_attn_kernel, bq=bq, bk=bk, group=group),
        grid=(batch * pairs, nq),
        in_specs=[q_spec, kv_spec, kv_spec],
        out_specs=q_spec,
        out_shape=jax.ShapeDtypeStruct((n, d), F32),
        scratch_shapes=[pltpu.VMEM((2 * bq, LANES), F32), pltpu.VMEM((2 * bq, LANES), F32),
                        pltpu.VMEM((in_flight, 2 * bq, bk), F32), pltpu.VMEM((in_flight, 2 * bq, 2 * bk), BF16),
                        pltpu.VMEM((in_flight, 2 * bq, 2 * bk), F32), pltpu.VMEM((in_flight, 2 * bq, bk), BF16)],
        compiler_params=_cparams(("parallel", "arbitrary")),
        name="sb_attn",
    )(q, k, v)


def _router_kernel(x_ref, w_ref, b_ref, idx_ref, gate_ref):
    logits = _dot_nt(w_ref[...], x_ref[...], HIGHEST) + b_ref[...]
    n_exp = logits.shape[0]
    e_idx = lax.broadcasted_iota(jnp.int32, logits.shape, 0)
    vals, idxs = [], []
    work = logits
    for _ in range(TOP_K):
        m = jnp.max(work, axis=0, keepdims=True)
        am = jnp.min(jnp.where(work == m, e_idx, n_exp), axis=0, keepdims=True)
        vals.append(m)
        idxs.append(am)
        work = jnp.where(e_idx == am, -jnp.inf, work)
    ex = [jnp.exp(vj - vals[0]) for vj in vals]
    inv = 1.0 / (ex[0] + ex[1] + ex[2] + ex[3])
    pad_i = jnp.zeros((8 - TOP_K, logits.shape[1]), jnp.int32)
    pad_f = jnp.zeros((8 - TOP_K, logits.shape[1]), F32)
    idx_ref[...] = jnp.concatenate(idxs + [pad_i], axis=0)
    gate_ref[...] = jnp.concatenate([e * inv for e in ex] + [pad_f], axis=0)


def _router(xf, w_router, b_router):
    n, d = xf.shape
    n_exp = w_router.shape[1]
    tm = min(ROW_TILE, n)
    out_spec = pl.BlockSpec((8, tm), lambda i: (0, i))
    idx, gate = pl.pallas_call(
        _router_kernel,
        grid=(n // tm,),
        in_specs=[pl.BlockSpec((tm, d), lambda i: (i, 0)), pl.BlockSpec((n_exp, d), lambda i: (0, 0)),
                  pl.BlockSpec((n_exp, 1), lambda i: (0, 0))],
        out_specs=[out_spec, out_spec],
        out_shape=[jax.ShapeDtypeStruct((8, n), jnp.int32), jax.ShapeDtypeStruct((8, n), F32)],
        compiler_params=_cparams(("parallel",)),
        name="moe_router",
    )(xf, w_router.T, b_router.reshape(n_exp, 1))
    return idx[:TOP_K].T, gate[:TOP_K].T


def _group_rows(top_idx, n_exp, rows):
    n = top_idx.shape[0]
    n_assign = n * TOP_K
    expert = top_idx.reshape(-1)
    order = jnp.argsort(expert, stable=True).astype(jnp.int32)
    counts = jnp.sum(expert[:, None] == jnp.arange(n_exp, dtype=jnp.int32)[None, :], axis=0, dtype=jnp.int32)
    padded = (counts + rows - 1) // rows * rows
    pad_end = jnp.cumsum(padded)
    pad_start = pad_end - padded
    n_blocks = n_assign // rows + n_exp
    block_pos = jnp.arange(n_blocks, dtype=jnp.int32)[:, None] * rows
    block_expert = jnp.minimum(jnp.sum(block_pos >= pad_end[None, :], axis=1, dtype=jnp.int32), n_exp - 1)
    pos = jnp.arange(n_blocks * rows, dtype=jnp.int32)
    blk = pos // rows
    past = pos[:, None] >= pad_end[None, :]
    n_past = jnp.sum(past, axis=1, dtype=jnp.int32)
    gap = jnp.sum(jnp.where(past, (padded - counts)[None, :], 0), axis=1, dtype=jnp.int32)
    n_real_end = jnp.sum(pos[:, None] >= (pad_start + counts)[None, :], axis=1, dtype=jnp.int32)
    valid = jnp.logical_and(n_real_end == n_past, pos < pad_end[-1])
    assign = order[jnp.clip(pos - gap, 0, n_assign - 1)]
    tok = assign // TOP_K
    slot = assign % TOP_K
    src = jnp.where(valid, tok, 0).reshape(n_blocks, 1, rows)
    dump = n_assign + (blk % 2) * rows + pos % rows
    dst = jnp.where(valid, slot * n + tok, dump).reshape(n_blocks, 1, rows)
    before_first = (n_assign + rows + jnp.arange(rows, dtype=jnp.int32)).reshape(1, 1, rows)
    src = jnp.concatenate([src, jnp.zeros((1, 1, rows), jnp.int32)], axis=0)
    dst_prev = jnp.concatenate([before_first, dst], axis=0)
    block_expert = jnp.concatenate([block_expert, block_expert[-1:]])
    n_used = (pad_end[-1] // rows).astype(jnp.int32).reshape(1)
    return src, dst_prev, block_expert, n_used


def _swiglu_split_kernel(w_ref, even_ref, odd_ref, glu_ref, lin_ref):
    w = w_ref[0].astype(BF16)
    glu_ref[0] = _dot(w, even_ref[...]).astype(BF16)
    lin_ref[0] = _dot(w, odd_ref[...]).astype(BF16)


def _swiglu_split(w1):
    n_exp, d, f2 = w1.shape
    cols = min(SPLIT_COLS, f2)
    r = jnp.arange(cols, dtype=jnp.int32)[:, None]
    c = jnp.arange(cols // 2, dtype=jnp.int32)[None, :]
    even = (r == 2 * c).astype(BF16)
    odd = (r == 2 * c + 1).astype(BF16)
    sel = pl.BlockSpec((cols, cols // 2), lambda e, j: (0, 0))
    out_spec = pl.BlockSpec((1, d, cols // 2), lambda e, j: (e, 0, j))
    out = jax.ShapeDtypeStruct((n_exp, d, f2 // 2), BF16)
    return pl.pallas_call(
        _swiglu_split_kernel,
        grid=(n_exp, f2 // cols),
        in_specs=[pl.BlockSpec((1, d, cols), lambda e, j: (e, 0, j)), sel, sel],
        out_specs=[out_spec, out_spec],
        out_shape=[out, out],
        compiler_params=_cparams(("parallel", "parallel")),
        name="swiglu_split",
    )(w1, even, odd)


def _expert_kernel(bexp_ref, nused_ref, src0_ref, src_next_ref, dst_prev_ref, x_hbm, w1g_ref, w1l_ref,
                   b1g_ref, b1l_ref, w2_ref, b2_ref, y_hbm, xbuf, ybuf, gsem, ssem, isem,
                   *, rows, n_tok_rows, chunks):
    del bexp_ref
    b = pl.program_id(0)
    n_used = nused_ref[0]
    slot = b % 2
    other = 1 - slot

    def gather_start(idx_ref, s, i, priority=0):
        pltpu.make_async_copy(x_hbm.at[pl.ds(pl.multiple_of(idx_ref[0, 0, i], chunks), chunks), :],
                              xbuf.at[s, pl.ds(i * chunks, chunks), :], gsem.at[s]).start(priority=priority)

    def scatter_start(s, i, priority=0):
        pltpu.make_async_copy(ybuf.at[s, pl.ds(i * chunks, chunks), :],
                              y_hbm.at[pl.ds(pl.multiple_of(dst_prev_ref[0, 0, i], chunks), chunks), :],
                              ssem.at[s]).start(priority=priority)

    def wait_gather(s):
        pltpu.make_async_copy(x_hbm.at[pl.ds(0, rows * chunks), :], xbuf.at[s], gsem.at[s]).wait()

    def wait_scatter(s):
        pltpu.make_async_copy(ybuf.at[s], y_hbm.at[pl.ds(0, rows * chunks), :], ssem.at[s]).wait()

    @pl.when(b == 0)
    def _():
        ybuf[...] = jnp.zeros_like(ybuf)
        for half in range(2):
            dump = pl.ds((n_tok_rows + half * rows) * chunks, rows * chunks)
            fill = pltpu.make_async_copy(ybuf.at[half], y_hbm.at[dump, :], isem.at[0])
            fill.start()
            fill.wait()

        def first(i, carry):
            gather_start(src0_ref, 0, i)
            return carry
        lax.fori_loop(0, rows, first, 0, unroll=DMA_UNROLL)

    @pl.when(jnp.logical_and(b >= 1, b <= n_used))
    def _():
        wait_scatter(slot)

    @pl.when(b <= n_used)
    def _():
        wait_gather(slot)

    @pl.when(b < n_used)
    def _():
        per_piece = rows // DMA_PIECES

        def copies(piece):
            for i in range(piece * per_piece, (piece + 1) * per_piece):
                gather_start(src_next_ref, other, i, priority=i % 2)
                scatter_start(other, i, priority=i % 2)

        xb = jnp.concatenate([xbuf[slot, pl.ds(c, rows, stride=chunks), :] for c in range(chunks)],
                             axis=1).astype(BF16)
        copies(0)
        glu = jnp.minimum(_dot(xb, w1g_ref[0]) + b1g_ref[0], SWIGLU_LIMIT)
        copies(1)
        lin = jnp.clip(_dot(xb, w1l_ref[0]) + b1l_ref[0], -SWIGLU_LIMIT, SWIGLU_LIMIT)
        copies(2)
        act = glu * _sigmoid(SWIGLU_ALPHA * glu) * (lin + 1.0)
        copies(3)
        y = _dot(act.astype(BF16), w2_ref[0]) + b2_ref[0]
        for c in range(chunks):
            ybuf[slot, pl.ds(c, rows, stride=chunks), :] = y[:, c * LANES:(c + 1) * LANES]

    @pl.when(b == n_used)
    def _():
        def last(i, carry):
            scatter_start(other, i)
            return carry
        lax.fori_loop(0, rows, last, 0, unroll=DMA_UNROLL)
        wait_scatter(other)


def _moe_experts(xf, src, dst_prev, block_expert, n_used, w1g, w1l, b1g, b1l, w2, b2):
    n, d = xf.shape
    n_steps, _, rows = src.shape
    n_exp, _, f = w1g.shape
    chunks = d // LANES
    idx_spec = lambda index_map: pl.BlockSpec((1, 1, rows), index_map, memory_space=pltpu.SMEM)
    per_expert = lambda shape: pl.BlockSpec((1,) + shape, lambda b, be, nu: (be[b], 0, 0))
    grid_spec = pltpu.PrefetchScalarGridSpec(
        num_scalar_prefetch=2,
        grid=(n_steps,),
        in_specs=[idx_spec(lambda b, be, nu: (0, 0, 0)),
                  idx_spec(lambda b, be, nu: (jnp.minimum(b + 1, n_steps - 1), 0, 0)),
                  idx_spec(lambda b, be, nu: (b, 0, 0)), pl.BlockSpec(memory_space=pl.ANY),
                  per_expert((d, f)), per_expert((d, f)), per_expert((1, f)), per_expert((1, f)),
                  per_expert((f, d)), per_expert((1, d))],
        out_specs=pl.BlockSpec(memory_space=pl.ANY),
        scratch_shapes=[pltpu.VMEM((2, rows * chunks, LANES), F32), pltpu.VMEM((2, rows * chunks, LANES), F32),
                        pltpu.SemaphoreType.DMA((2,)), pltpu.SemaphoreType.DMA((2,)),
                        pltpu.SemaphoreType.DMA((1,))],
    )
    y = pl.pallas_call(
        functools.partial(_expert_kernel, rows=rows, n_tok_rows=TOP_K * n, chunks=chunks),
        grid_spec=grid_spec,
        out_shape=jax.ShapeDtypeStruct(((TOP_K * n + 2 * rows) * chunks, LANES), F32),
        compiler_params=_cparams(("arbitrary",)),
        name="moe_experts",
    )(block_expert, n_used, src * chunks, src * chunks, dst_prev * chunks, xf.reshape(n * chunks, LANES),
      w1g, w1l, b1g, b1l, w2, b2)
    return y.reshape(TOP_K * n + 2 * rows, d)


def _combine_kernel(x_ref, gate_ref, y0_ref, y1_ref, y2_ref, y3_ref, lg_ref, lb_ref, out_ref):
    gate = gate_ref[...]
    f = (gate[:, 0:1] * y0_ref[...] + gate[:, 1:2] * y1_ref[...]
         + gate[:, 2:3] * y2_ref[...] + gate[:, 3:4] * y3_ref[...])
    out_ref[...] = _layer_norm(ALPHA * x_ref[...] + f, lg_ref[...], lb_ref[...])


def _moe_combine(xf, gate, y, ln_g, ln_b):
    n, d = xf.shape
    tm = min(ROW_TILE, n)
    tiles = n // tm
    row_tile = pl.BlockSpec((tm, d), lambda i: (i, 0))
    slot_tile = lambda j: pl.BlockSpec((tm, d), lambda i: (j * tiles + i, 0))
    vec = pl.BlockSpec((1, d), lambda i: (0, 0))
    return pl.pallas_call(
        _combine_kernel,
        grid=(tiles,),
        in_specs=[row_tile, pl.BlockSpec((tm, TOP_K), lambda i: (i, 0))] + [slot_tile(j) for j in range(TOP_K)]
                 + [vec, vec],
        out_specs=row_tile,
        out_shape=jax.ShapeDtypeStruct((n, d), F32),
        compiler_params=_cparams(("parallel",)),
        name="moe_combine",
    )(xf, gate, y, y, y, y, ln_g.reshape(1, d), ln_b.reshape(1, d))


def _moe(xf, w_router, b_router, w1g, w1l, b1, w2, b2, ln_g, ln_b):
    n, d = xf.shape
    n_exp = w_router.shape[1]
    rows = min(EXPERT_ROWS, n)
    top_idx, gate = _router(xf, w_router, b_router)
    src, dst_prev, block_expert, n_used = _group_rows(top_idx, n_exp, rows)
    y = _moe_experts(xf, src, dst_prev, block_expert, n_used, w1g, w1l, b1[:, None, 0::2], b1[:, None, 1::2],
                     w2.astype(BF16), b2[:, None, :])
    return _moe_combine(xf, gate, y, ln_g, ln_b)


def kernel(x, rw_mix, rw_w_rkv, rw_w0, rw_w1, rw_w2, rw_a0, rw_a1, rw_a2, rw_g1, rw_g2, rw_k_k, rw_k_a, rw_r_k,
           rw_lnx_g, rw_lnx_b, rw_w_o, sb_w_qkv, sb_w_o, moe_w_router, moe_b_router, moe_w1, moe_b1, moe_w2,
           moe_b2, ln_g, ln_b):
    batch, seq_len, d = x.shape
    xf = x.reshape(batch * seq_len, d)
    n_layers, n_exp = moe_w1.shape[:2]
    w1g, w1l = _swiglu_split(moe_w1.reshape((n_layers * n_exp,) + moe_w1.shape[2:]))
    w1g = w1g.reshape((n_layers, n_exp) + w1g.shape[1:])
    w1l = w1l.reshape((n_layers, n_exp) + w1l.shape[1:])
    for i in range(DEPTH):
        j = i // 2
        if i % 2 == 0:
            r, k, v, ld, a, g = _rwkv_proj(xf, seq_len, rw_mix[j], rw_w_rkv[j], rw_w0[j], rw_w1[j], rw_w2[j],
                                           rw_a0[j], rw_a1[j], rw_a2[j], rw_g1[j], rw_g2[j])
            o = _rwkv_scan(r, k, v, ld, a, seq_len, rw_k_k[j], rw_k_a[j], rw_r_k[j], rw_lnx_g[j], rw_lnx_b[j])
            xf = _proj_ln(xf, o, g, rw_w_o[j], ln_g[i, 0], ln_b[i, 0])
        else:
            q, k, v = _sb_qkv(xf, sb_w_qkv[j])
            o = _sb_attn(q, k, v, seq_len)
            xf = _proj_ln(xf, o, None, sb_w_o[j], ln_g[i, 0], ln_b[i, 0])
        xf = _moe(xf, moe_w_router[i], moe_b_router[i], w1g[i], w1l[i], moe_b1[i], moe_w2[i], moe_b2[i],
                  ln_g[i, 1], ln_b[i, 1])
    return xf.reshape(batch, seq_len, d)
```

```python
import functools

import jax
import jax.numpy as jnp
from jax import lax
from jax.experimental import pallas as pl
from jax.experimental.pallas import tpu as pltpu

F32 = jnp.float32
BF16 = jnp.bfloat16
HIGHEST = lax.Precision.HIGHEST

DEPTH = 2
ALPHA = (2 * DEPTH) ** 0.25
LN_EPS = 1e-5
GN_EPS = 64e-5
HEAD_DIM = 64
LANES = 128
TOP_K = 4
SWIGLU_LIMIT = 7.0
SWIGLU_ALPHA = 1.702

ROW_TILE = 256
SCAN_CHUNK = 64
SCAN_ROWS = 1024
SCAN_UNROLL = 16
SB_Q_BLOCK = 256
SB_K_BLOCK = LANES
SB_GROUP = 2
SB_EXP_ZERO = -104.0
SB_ROW_CHUNK = 64
EXPERT_ROWS = 256
DMA_UNROLL = 8
DMA_PIECES = 4
SPLIT_COLS = 512
VMEM_LIMIT = 56 * 1024 * 1024


def _cparams(semantics):
    return pltpu.CompilerParams(dimension_semantics=semantics, vmem_limit_bytes=VMEM_LIMIT)


def _dot(a, b, precision=None):
    return jnp.dot(a, b, preferred_element_type=F32, precision=precision)


def _dot_nt(a, b, precision=None):
    return lax.dot_general(a, b, (((1,), (1,)), ((), ())), preferred_element_type=F32, precision=precision)


def _dot_tn(a, b, precision=None):
    return lax.dot_general(a, b, (((0,), (0,)), ((), ())), preferred_element_type=F32, precision=precision)


def _split2(a):
    hi = a.astype(BF16)
    return hi, (a - hi.astype(F32)).astype(BF16)


def _split3(a):
    hi = a.astype(BF16)
    rest = a - hi.astype(F32)
    mid = rest.astype(BF16)
    return hi, mid, (rest - mid.astype(F32)).astype(BF16)


def _mm3(a, b, dot=_dot):
    a_axis = 0 if dot is _dot_tn else 1
    b_axis = 1 if dot is _dot_nt else 0
    return dot(jnp.concatenate([a[0], a[0], a[1]], axis=a_axis), jnp.concatenate([b[0], b[1], b[0]], axis=b_axis))


def _softplus(u):
    return jnp.maximum(u, 0.0) + jnp.log(1.0 + jnp.exp(-jnp.abs(u)))


def _sigmoid(u):
    return 1.0 / (1.0 + jnp.exp(-u))


def _layer_norm(y, g, b):
    mu = jnp.mean(y, axis=-1, keepdims=True)
    d = y - mu
    var = jnp.mean(d * d, axis=-1, keepdims=True)
    return d * lax.rsqrt(var + LN_EPS) * g + b


def _head_block_ones():
    r = lax.broadcasted_iota(jnp.int32, (LANES, LANES), 0) // HEAD_DIM
    c = lax.broadcasted_iota(jnp.int32, (LANES, LANES), 1) // HEAD_DIM
    return (r == c).astype(BF16)


def _head_sums(x, head_ones):
    m = x.shape[0]
    s = _dot(jnp.concatenate(_split3(x), axis=0), head_ones)
    return s[:m] + s[m:2 * m] + s[2 * m:]


def _rwkv_proj_kernel(x_ref, xp_ref, mix_ref, wrkv_ref, w0_ref, w1_ref, w2_ref, a0_ref, a1_ref, a2_ref,
                      g1_ref, g2_ref, r_ref, k_ref, v_ref, ld_ref, a_ref, g_ref, *, tiles_per_seq):
    i = pl.program_id(0)
    x = x_ref[...]
    prev_last = jnp.where(i % tiles_per_seq == 0, 0.0, xp_ref[7:8, :])
    row = lax.broadcasted_iota(jnp.int32, x.shape, 0)
    x_prev = jnp.where(row == 0, prev_last, pltpu.roll(x, 1, 0))
    xx = x_prev - x

    def mixed(j):
        return (x + xx * mix_ref[j:j + 1, :]).astype(BF16)

    r_ref[...] = _dot(mixed(0), wrkv_ref[0])
    w = w0_ref[...] + _dot(jnp.tanh(_dot(mixed(1), w1_ref[...])).astype(BF16), w2_ref[...])
    ld_ref[...] = -jnp.exp(-_softplus(-w) - 0.5)
    k_ref[...] = _dot(mixed(2), wrkv_ref[1])
    v_ref[...] = _dot(mixed(3), wrkv_ref[2])
    a_ref[...] = _sigmoid(a0_ref[...] + _dot(_dot(mixed(4), a1_ref[...]).astype(BF16), a2_ref[...]))
    g_ref[...] = _dot(_sigmoid(_dot(mixed(5), g1_ref[...])).astype(BF16), g2_ref[...])


def _rwkv_proj(xf, seq_len, mix, w_rkv, w0, w1, w2, a0, a1, a2, g1, g2):
    n, d = xf.shape
    tm = min(ROW_TILE, seq_len)
    full = lambda shape: pl.BlockSpec(shape, lambda i: (0,) * len(shape))
    row_tile = pl.BlockSpec((tm, d), lambda i: (i, 0))
    prev_rows = pl.BlockSpec((8, d), lambda i: (jnp.maximum(i * (tm // 8) - 1, 0), 0))
    mix8 = jnp.concatenate([mix, jnp.zeros((2, d), F32)], axis=0)
    out = jax.ShapeDtypeStruct((n, d), F32)
    return pl.pallas_call(
        functools.partial(_rwkv_proj_kernel, tiles_per_seq=seq_len // tm),
        grid=(n // tm,),
        in_specs=[row_tile, prev_rows, full((8, d)), full(w_rkv.shape), full((1, d)), full(w1.shape),
                  full(w2.shape), full((1, d)), full(a1.shape), full(a2.shape), full(g1.shape), full(g2.shape)],
        out_specs=[row_tile] * 6,
        out_shape=[out] * 6,
        compiler_params=_cparams(("parallel",)),
        name="rwkv_proj",
    )(xf, xf, mix8, w_rkv.astype(BF16), w0.reshape(1, d), w1.astype(BF16), w2.astype(BF16),
      a0.reshape(1, d), a1.astype(BF16), a2.astype(BF16), g1.astype(BF16), g2.astype(BF16))


def _rwkv_scan_kernel(r_ref, k_ref, v_ref, ld_ref, a_ref, kk_ref, ka_ref, rk_ref, lg_ref, lb_ref,
                      o_ref, s_ref, rhat_ref, o0_ref, p_ref, q_ref, dcol_ref, *, chunk, n_chunks, unroll):
    L = chunk

    @pl.when(pl.program_id(1) == 0)
    def _():
        s_ref[...] = jnp.zeros_like(s_ref)

    head0 = lax.broadcasted_iota(jnp.int32, (1, LANES), 1) < HEAD_DIM
    row = lax.broadcasted_iota(jnp.int32, (2 * L, 2 * L), 0)
    col = lax.broadcasted_iota(jnp.int32, (2 * L, 2 * L), 1)
    same_head = (row // L) == (col // L)
    lower_strict = jnp.logical_and(same_head, col < row)
    lower_incl = jnp.logical_and(same_head, col <= row)
    eye = (row == col).astype(F32)
    cum_rows = (lax.broadcasted_iota(jnp.int32, (L, L), 1)
                <= lax.broadcasted_iota(jnp.int32, (L, L), 0)).astype(BF16)
    head_ones = _head_block_ones()
    ones_ln = jnp.ones((L, LANES), BF16)

    def stack(t):
        return jnp.concatenate([jnp.where(head0, t, 0.0), jnp.where(head0, 0.0, t)], axis=0)

    def modified_key(k, a):
        return k * (1.0 + (a - 1.0) * ka_ref[...])

    def prepare(ci):
        sl = pl.ds(pl.multiple_of(ci * L, L), L)
        r, k, v, ld, a = r_ref[sl, :], k_ref[sl, :], v_ref[sl, :], ld_ref[sl, :], a_ref[sl, :]
        kk = k * kk_ref[...]
        kk_sq = _head_sums(kk * kk, head_ones)
        ld3 = jnp.concatenate(_split3(ld), axis=1)
        c3 = _dot(cum_rows, ld3)
        tot3 = _dot_tn(ld3, ones_ln)
        yield
        kk = kk / jnp.maximum(jnp.sqrt(kk_sq), 1e-12)
        k2 = modified_key(k, a)
        a_vec, b_vec = -kk, kk * a
        c = c3[:, :LANES] + c3[:, LANES:2 * LANES] + c3[:, 2 * LANES:]
        dcol_ref[ci] = jnp.exp(tot3[:LANES] + tot3[LANES:2 * LANES] + tot3[2 * LANES:])
        to_end = jnp.exp(c[L - 1:L, :] - c)
        inv = jnp.exp(-c)
        a_st = stack(a_vec * jnp.exp(c - ld))
        r_st = stack(r * jnp.exp(c))
        v_sp = _split2(stack(v))
        bh_sp = _split2(stack(b_vec * to_end))
        kh_sp = _split2(stack(k2 * to_end))
        ar = _split2(jnp.concatenate([a_st, r_st], axis=0))
        bk = _split2(jnp.concatenate([stack(b_vec * inv), stack(k2 * inv)], axis=0))
        prod = _mm3(ar, bk, _dot_nt)
        kv = _mm3(kh_sp, v_sp, _dot_tn)
        yield
        m_ab = jnp.where(lower_strict, prod[:2 * L, :2 * L], 0.0)
        m_ak = jnp.where(lower_strict, prod[:2 * L, 2 * L:], 0.0)
        a_rb = jnp.where(lower_incl, prod[2 * L:, :2 * L], 0.0)
        a_rk = jnp.where(lower_incl, prod[2 * L:, 2 * L:], 0.0)
        w0 = _mm3(_split2(m_ak), v_sp)
        rkv = _mm3(_split2(a_rk), v_sp)
        t_inv = eye
        w = 1
        while w < L:
            below = jnp.logical_and((row // (2 * w)) == (col // (2 * w)),
                                    jnp.logical_and((row % (2 * w)) >= w, (col % (2 * w)) < w))
            t_bf = t_inv.astype(BF16)
            tm = _dot(t_bf, jnp.where(below, m_ab, 0.0).astype(BF16))
            yield
            tmt = _dot(tm.astype(BF16), t_bf)
            yield
            t_inv = t_inv + tmt
            w *= 2
        resid = (eye - t_inv) + _mm3(_split2(m_ab), _split2(t_inv))
        yield
        t_inv = t_inv + _dot(t_inv.astype(BF16), resid.astype(BF16))
        yield
        au = _mm3(_split2(t_inv), _split2(jnp.concatenate([a_st, w0], axis=1)))
        yield
        au_sp = _split2(au)
        ro = _mm3(_split2(a_rb), au_sp)
        pq = _mm3(bh_sp, au_sp, _dot_tn)
        yield
        rhat_ref[ci] = r_st + ro[:, :LANES]
        o0_ref[ci] = ro[:, LANES:] + rkv
        p_ref[ci] = pq[:, :LANES]
        q_ref[ci] = pq[:, LANES:] + kv

    def prepare_group(gi, carry):
        chunks = [prepare(gi * unroll + u) for u in range(unroll)]
        while chunks:
            for g in chunks:
                if next(g, StopIteration) is StopIteration:
                    chunks = []
        return carry

    lax.fori_loop(0, n_chunks // unroll, prepare_group, 0)

    s = s_ref[...]
    for ci in range(n_chunks):
        s_sp = _split2(s)
        o_st = _mm3(_split2(rhat_ref[ci]), s_sp) + o0_ref[ci]
        o_ref[ci * L:(ci + 1) * L, :] = o_st[:L] + o_st[L:]
        s = dcol_ref[ci] * s + _mm3(_split2(p_ref[ci]), s_sp) + q_ref[ci]
    s_ref[...] = s

    o = o_ref[...]
    inv_n = 1.0 / HEAD_DIM
    mu = _head_sums(o, head_ones) * inv_n
    d = o - mu
    var = _head_sums(d * d, head_ones) * inv_n
    r, v = r_ref[...], v_ref[...]
    k2 = modified_key(k_ref[...], a_ref[...])
    bonus = _head_sums(r * k2 * rk_ref[...], head_ones) * v
    o_ref[...] = d * lax.rsqrt(var + GN_EPS) * lg_ref[...] + lb_ref[...] + bonus


def _rwkv_scan(r, k, v, ld, a, seq_len, k_k, k_a, r_k, lnx_g, lnx_b):
    n, d = r.shape
    batch = n // seq_len
    pairs = d // LANES
    rows = min(SCAN_ROWS, seq_len)
    steps = seq_len // rows
    n_chunks = rows // SCAN_CHUNK
    seq = pl.BlockSpec((rows, LANES), lambda p, c: ((p // pairs) * steps + c, p % pairs))
    par = pl.BlockSpec((1, LANES), lambda p, c: (0, p % pairs))
    per_chunk = pltpu.VMEM((n_chunks, 2 * SCAN_CHUNK, LANES), F32)
    return pl.pallas_call(
        functools.partial(_rwkv_scan_kernel, chunk=SCAN_CHUNK, n_chunks=n_chunks,
                          unroll=min(SCAN_UNROLL, n_chunks)),
        grid=(batch * pairs, steps),
        in_specs=[seq] * 5 + [par] * 5,
        out_specs=seq,
        out_shape=jax.ShapeDtypeStruct((n, d), F32),
        scratch_shapes=[pltpu.VMEM((LANES, LANES), F32)] + [per_chunk] * 5,
        compiler_params=_cparams(("parallel", "arbitrary")),
        name="rwkv_scan",
    )(r, k, v, ld, a, k_k.reshape(1, d), k_a.reshape(1, d), r_k.reshape(1, d),
      lnx_g.reshape(1, d), lnx_b.reshape(1, d))


def _proj_ln_kernel(*refs, gated):
    if gated:
        x_ref, o_ref, g_ref, w_ref, lg_ref, lb_ref, out_ref, tiles_ref = refs
        h = o_ref[...] * g_ref[...]
    else:
        x_ref, o_ref, w_ref, lg_ref, lb_ref, out_ref, tiles_ref = refs
        h = o_ref[...]
    y = ALPHA * x_ref[...] + _dot(h.astype(BF16), w_ref[...])
    out = _layer_norm(y, lg_ref[...], lb_ref[...])
    out_ref[...] = out
    tm, d = out.shape
    chunks = d // LANES
    for c in range(chunks):
        tiles_ref[pl.ds(c, tm, stride=chunks), :] = out[:, c * LANES:(c + 1) * LANES]


def _proj_ln(xf, o, g, w_o, ln_g, ln_b):
    n, d = xf.shape
    tm = min(ROW_TILE, n)
    row_tile = pl.BlockSpec((tm, d), lambda i: (i, 0))
    full = lambda shape: pl.BlockSpec(shape, lambda i: (0,) * len(shape))
    gated = g is not None
    acts = (xf, o, g) if gated else (xf, o)
    return pl.pallas_call(
        functools.partial(_proj_ln_kernel, gated=gated),
        grid=(n // tm,),
        in_specs=[row_tile] * len(acts) + [full((d, d)), full((1, d)), full((1, d))],
        out_specs=[row_tile, pl.BlockSpec((tm * (d // LANES), LANES), lambda i: (i, 0))],
        out_shape=[jax.ShapeDtypeStruct((n, d), F32), jax.ShapeDtypeStruct((n * (d // LANES), LANES), F32)],
        compiler_params=_cparams(("parallel",)),
        name="proj_ln",
    )(*acts, w_o.astype(BF16), ln_g.reshape(1, d), ln_b.reshape(1, d))


def _sb_qkv_kernel(x_ref, w_ref, q_ref, k_ref, v_ref):
    d = x_ref.shape[1]
    qkv = _dot(x_ref[...].astype(BF16), w_ref[...])
    q_ref[...] = (qkv[:, :d] * (HEAD_DIM ** -0.5)).astype(BF16)
    k_ref[...] = qkv[:, d:2 * d].astype(BF16)
    v_ref[...] = qkv[:, 2 * d:].astype(BF16)


def _sb_qkv(xf, w_qkv):
    n, d = xf.shape
    tm = min(ROW_TILE, n)
    row_tile = pl.BlockSpec((tm, d), lambda i: (i, 0))
    out = jax.ShapeDtypeStruct((n, d), BF16)
    return pl.pallas_call(
        _sb_qkv_kernel,
        grid=(n // tm,),
        in_specs=[row_tile, pl.BlockSpec((d, 3 * d), lambda i: (0, 0))],
        out_specs=[row_tile] * 3,
        out_shape=[out] * 3,
        compiler_params=_cparams(("parallel",)),
        name="sb_qkv",
    )(xf, w_qkv.astype(BF16))


def _sb_attn_kernel(q_ref, k_ref, v_ref, o_ref, acc_ref, run_ref, ls_buf, lk_buf, sum_buf, att_buf,
                    *, bq, bk, group):
    i = pl.program_id(1)
    ratio = bq // bk
    m = 2 * bq
    head0 = lax.broadcasted_iota(jnp.int32, (1, LANES), 1) < HEAD_DIM
    q = q_ref[...]
    zero = jnp.zeros_like(q)
    q2 = jnp.concatenate([jnp.where(head0, q, zero), jnp.where(head0, zero, q)], axis=0)
    kr = lax.broadcasted_iota(jnp.int32, (2 * bk, 2 * bk), 0) % bk
    kc = lax.broadcasted_iota(jnp.int32, (2 * bk, 2 * bk), 1)
    suffix = jnp.logical_or(kr > kc, kc >= bk).astype(BF16)
    acc_ref[...] = jnp.zeros_like(acc_ref)
    run_ref[...] = jnp.zeros_like(run_ref)
    row_chunks = [slice(c, c + SB_ROW_CHUNK) for c in range(0, m, SB_ROW_CHUNK)]

    def causal(j, rs):
        q_pos = i * bq + (lax.broadcasted_iota(jnp.int32, (SB_ROW_CHUNK, bk), 0) + rs.start) % bq
        return (j * bk + lax.broadcasted_iota(jnp.int32, (SB_ROW_CHUNK, bk), 1)) < q_pos

    def stages(j, masked, u):
        ks = pl.ds(pl.multiple_of(j * bk, bk), bk)

        def scores():
            ls_buf[u] = _dot_nt(q2, k_ref[ks, :])

        def log_terms():
            for rs in row_chunks:
                z = ls_buf[u, rs, :]
                log_keep = -_softplus(z)
                if masked:
                    log_keep = jnp.where(causal(j, rs), log_keep, 0.0)
                hi, lo = _split2(log_keep)
                lk_buf[u, rs, :bk] = hi
                lk_buf[u, rs, bk:] = lo
                ls_buf[u, rs, :] = z + log_keep

        def suffix_sums():
            sum_buf[u] = _dot(lk_buf[u], suffix)

        def weights():
            for rs in row_chunks:
                run = run_ref[rs, :]
                att = jnp.exp(ls_buf[u, rs, :] + run + sum_buf[u, rs, :bk])
                if masked:
                    att = jnp.where(causal(j, rs), att, 0.0)
                att_buf[u, rs, :] = att.astype(BF16)
                run_ref[rs, :] = run + sum_buf[u, rs, bk:]

        def output():
            acc_ref[...] += _dot(att_buf[u], v_ref[ks, :])

        return (scores, log_terms, suffix_sums, weights, output)

    def run_tiles(tiles, masked):
        per_tile = [stages(j, masked, u) for u, j in enumerate(tiles)]
        for stage in zip(*per_tile):
            for emit in stage:
                emit()

    run_tiles([(i + 1) * ratio - 1 - u for u in range(ratio)], True)
    n_off = i * ratio

    def live(step):
        return jnp.logical_and(step < n_off // group, jnp.max(run_ref[...]) > SB_EXP_ZERO)

    def body(step):
        run_tiles([n_off - 1 - step * group - u for u in range(group)], False)
        return step + 1

    lax.while_loop(live, body, 0)
    o_ref[...] = jnp.where(head0, acc_ref[:bq, :], acc_ref[bq:, :])


def _sb_attn(q, k, v, seq_len):
    n, d = q.shape
    batch = n // seq_len
    pairs = d // LANES
    bq = min(SB_Q_BLOCK, seq_len)
    bk = SB_K_BLOCK
    nq = seq_len // bq
    ratio = bq // bk
    group = min(SB_GROUP, ratio)
    assert ratio % group == 0
    in_flight = ratio
    q_spec = pl.BlockSpec((bq, LANES), lambda p, i: ((p // pairs) * nq + i, p % pairs))
    kv_spec = pl.BlockSpec((seq_len, LANES), lambda p, i: (p // pairs, p % pairs))
    return pl.pallas_call(
        functools.partial(_sb_attn_kernel, bq=bq, bk=bk, group=group),
        grid=(batch * pairs, nq),
        in_specs=[q_spec, kv_spec, kv_spec],
        out_specs=q_spec,
        out_shape=jax.ShapeDtypeStruct((n, d), F32),
        scratch_shapes=[pltpu.VMEM((2 * bq, LANES), F32), pltpu.VMEM((2 * bq, LANES), F32),
                        pltpu.VMEM((in_flight, 2 * bq, bk), F32), pltpu.VMEM((in_flight, 2 * bq, 2 * bk), BF16),
                        pltpu.VMEM((in_flight, 2 * bq, 2 * bk), F32), pltpu.VMEM((in_flight, 2 * bq, bk), BF16)],
        compiler_params=_cparams(("parallel", "arbitrary")),
        name="sb_attn",
    )(q, k, v)


def _router_kernel(x_ref, w_ref, b_ref, idx_ref, gate_ref):
    logits = _dot_nt(w_ref[...], x_ref[...], HIGHEST) + b_ref[...]
    n_exp = logits.shape[0]
    e_idx = lax.broadcasted_iota(jnp.int32, logits.shape, 0)
    vals, idxs = [], []
    work = logits
    for _ in range(TOP_K):
        m = jnp.max(work, axis=0, keepdims=True)
        am = jnp.min(jnp.where(work == m, e_idx, n_exp), axis=0, keepdims=True)
        vals.append(m)
        idxs.append(am)
        work = jnp.where(e_idx == am, -jnp.inf, work)
    ex = [jnp.exp(vj - vals[0]) for vj in vals]
    inv = 1.0 / (ex[0] + ex[1] + ex[2] + ex[3])
    pad_i = jnp.zeros((8 - TOP_K, logits.shape[1]), jnp.int32)
    pad_f = jnp.zeros((8 - TOP_K, logits.shape[1]), F32)
    idx_ref[...] = jnp.concatenate(idxs + [pad_i], axis=0)
    gate_ref[...] = jnp.concatenate([e * inv for e in ex] + [pad_f], axis=0)


def _router(xf, w_router, b_router):
    n, d = xf.shape
    n_exp = w_router.shape[1]
    tm = min(ROW_TILE, n)
    out_spec = pl.BlockSpec((8, tm), lambda i: (0, i))
    idx, gate = pl.pallas_call(
        _router_kernel,
        grid=(n // tm,),
        in_specs=[pl.BlockSpec((tm, d), lambda i: (i, 0)), pl.BlockSpec((n_exp, d), lambda i: (0, 0)),
                  pl.BlockSpec((n_exp, 1), lambda i: (0, 0))],
        out_specs=[out_spec, out_spec],
        out_shape=[jax.ShapeDtypeStruct((8, n), jnp.int32), jax.ShapeDtypeStruct((8, n), F32)],
        compiler_params=_cparams(("parallel",)),
        name="moe_router",
    )(xf, w_router.T, b_router.reshape(n_exp, 1))
    return idx[:TOP_K].T, gate[:TOP_K].T


def _group_rows(top_idx, n_exp, rows):
    n = top_idx.shape[0]
    n_assign = n * TOP_K
    expert = top_idx.reshape(-1)
    order = jnp.argsort(expert, stable=True).astype(jnp.int32)
    counts = jnp.sum(expert[:, None] == jnp.arange(n_exp, dtype=jnp.int32)[None, :], axis=0, dtype=jnp.int32)
    padded = (counts + rows - 1) // rows * rows
    pad_end = jnp.cumsum(padded)
    pad_start = pad_end - padded
    n_blocks = n_assign // rows + n_exp
    block_pos = jnp.arange(n_blocks, dtype=jnp.int32)[:, None] * rows
    block_expert = jnp.minimum(jnp.sum(block_pos >= pad_end[None, :], axis=1, dtype=jnp.int32), n_exp - 1)
    pos = jnp.arange(n_blocks * rows, dtype=jnp.int32)
    blk = pos // rows
    past = pos[:, None] >= pad_end[None, :]
    n_past = jnp.sum(past, axis=1, dtype=jnp.int32)
    gap = jnp.sum(jnp.where(past, (padded - counts)[None, :], 0), axis=1, dtype=jnp.int32)
    n_real_end = jnp.sum(pos[:, None] >= (pad_start + counts)[None, :], axis=1, dtype=jnp.int32)
    valid = jnp.logical_and(n_real_end == n_past, pos < pad_end[-1])
    assign = order[jnp.clip(pos - gap, 0, n_assign - 1)]
    tok = assign // TOP_K
    slot = assign % TOP_K
    src = jnp.where(valid, tok, 0).reshape(n_blocks, 1, rows)
    dump = n_assign + (blk % 2) * rows + pos % rows
    dst = jnp.where(valid, slot * n + tok, dump).reshape(n_blocks, 1, rows)
    before_first = (n_assign + rows + jnp.arange(rows, dtype=jnp.int32)).reshape(1, 1, rows)
    src = jnp.concatenate([src, jnp.zeros((1, 1, rows), jnp.int32)], axis=0)
    dst_prev = jnp.concatenate([before_first, dst], axis=0)
    block_expert = jnp.concatenate([block_expert, block_expert[-1:]])
    n_used = (pad_end[-1] // rows).astype(jnp.int32).reshape(1)
    return src, dst_prev, block_expert, n_used


def _swiglu_split_kernel(w_ref, even_ref, odd_ref, glu_ref, lin_ref):
    w = w_ref[0].astype(BF16)
    glu_ref[0] = _dot(w, even_ref[...]).astype(BF16)
    lin_ref[0] = _dot(w, odd_ref[...]).astype(BF16)


def _swiglu_split(w1):
    n_exp, d, f2 = w1.shape
    cols = min(SPLIT_COLS, f2)
    r = jnp.arange(cols, dtype=jnp.int32)[:, None]
    c = jnp.arange(cols // 2, dtype=jnp.int32)[None, :]
    even = (r == 2 * c).astype(BF16)
    odd = (r == 2 * c + 1).astype(BF16)
    sel = pl.BlockSpec((cols, cols // 2), lambda e, j: (0, 0))
    out_spec = pl.BlockSpec((1, d, cols // 2), lambda e, j: (e, 0, j))
    out = jax.ShapeDtypeStruct((n_exp, d, f2 // 2), BF16)
    return pl.pallas_call(
        _swiglu_split_kernel,
        grid=(n_exp, f2 // cols),
        in_specs=[pl.BlockSpec((1, d, cols), lambda e, j: (e, 0, j)), sel, sel],
        out_specs=[out_spec, out_spec],
        out_shape=[out, out],
        compiler_params=_cparams(("parallel", "parallel")),
        name="swiglu_split",
    )(w1, even, odd)


def _expert_kernel(bexp_ref, nused_ref, src0_ref, src_next_ref, dst_prev_ref, x_hbm, w1g_ref, w1l_ref,
                   b1g_ref, b1l_ref, w2_ref, b2_ref, y_hbm, xbuf, ybuf, gsem, ssem, isem,
                   *, rows, n_tok_rows, chunks):
    del bexp_ref
    b = pl.program_id(0)
    n_used = nused_ref[0]
    slot = b % 2
    other = 1 - slot

    def gather_start(idx_ref, s, i, priority=0):
        pltpu.make_async_copy(x_hbm.at[pl.ds(pl.multiple_of(idx_ref[0, 0, i], chunks), chunks), :],
                              xbuf.at[s, pl.ds(i * chunks, chunks), :], gsem.at[s]).start(priority=priority)

    def scatter_start(s, i, priority=0):
        pltpu.make_async_copy(ybuf.at[s, pl.ds(i * chunks, chunks), :],
                              y_hbm.at[pl.ds(pl.multiple_of(dst_prev_ref[0, 0, i], chunks), chunks), :],
                              ssem.at[s]).start(priority=priority)

    def wait_gather(s):
        pltpu.make_async_copy(x_hbm.at[pl.ds(0, rows * chunks), :], xbuf.at[s], gsem.at[s]).wait()

    def wait_scatter(s):
        pltpu.make_async_copy(ybuf.at[s], y_hbm.at[pl.ds(0, rows * chunks), :], ssem.at[s]).wait()

    @pl.when(b == 0)
    def _():
        ybuf[...] = jnp.zeros_like(ybuf)
        for half in range(2):
            dump = pl.ds((n_tok_rows + half * rows) * chunks, rows * chunks)
            fill = pltpu.make_async_copy(ybuf.at[half], y_hbm.at[dump, :], isem.at[0])
            fill.start()
            fill.wait()

        def first(i, carry):
            gather_start(src0_ref, 0, i)
            return carry
        lax.fori_loop(0, rows, first, 0, unroll=DMA_UNROLL)

    @pl.when(jnp.logical_and(b >= 1, b <= n_used))
    def _():
        wait_scatter(slot)

    @pl.when(b <= n_used)
    def _():
        wait_gather(slot)

    @pl.when(b < n_used)
    def _():
        per_piece = rows // DMA_PIECES

        def copies(piece):
            for i in range(piece * per_piece, (piece + 1) * per_piece):
                gather_start(src_next_ref, other, i, priority=i % 2)
                scatter_start(other, i, priority=i % 2)

        xb = jnp.concatenate([xbuf[slot, pl.ds(c, rows, stride=chunks), :] for c in range(chunks)],
                             axis=1).astype(BF16)
        copies(0)
        glu = jnp.minimum(_dot(xb, w1g_ref[0]) + b1g_ref[0], SWIGLU_LIMIT)
        copies(1)
        lin = jnp.clip(_dot(xb, w1l_ref[0]) + b1l_ref[0], -SWIGLU_LIMIT, SWIGLU_LIMIT)
        copies(2)
        act = glu * _sigmoid(SWIGLU_ALPHA * glu) * (lin + 1.0)
        copies(3)
        y = _dot(act.astype(BF16), w2_ref[0]) + b2_ref[0]
        for c in range(chunks):
            ybuf[slot, pl.ds(c, rows, stride=chunks), :] = y[:, c * LANES:(c + 1) * LANES]

    @pl.when(b == n_used)
    def _():
        def last(i, carry):
            scatter_start(other, i)
            return carry
        lax.fori_loop(0, rows, last, 0, unroll=DMA_UNROLL)
        wait_scatter(other)


def _moe_experts(x_tiles, src, dst_prev, block_expert, n_used, w1g, w1l, b1g, b1l, w2, b2):
    n_steps, _, rows = src.shape
    n_exp, d, f = w1g.shape
    chunks = d // LANES
    n = x_tiles.shape[0] // chunks
    idx_spec = lambda index_map: pl.BlockSpec((1, 1, rows), index_map, memory_space=pltpu.SMEM)
    per_expert = lambda shape: pl.BlockSpec((1,) + shape, lambda b, be, nu: (be[b], 0, 0))
    grid_spec = pltpu.PrefetchScalarGridSpec(
        num_scalar_prefetch=2,
        grid=(n_steps,),
        in_specs=[idx_spec(lambda b, be, nu: (0, 0, 0)),
                  idx_spec(lambda b, be, nu: (jnp.minimum(b + 1, n_steps - 1), 0, 0)),
                  idx_spec(lambda b, be, nu: (b, 0, 0)), pl.BlockSpec(memory_space=pl.ANY),
                  per_expert((d, f)), per_expert((d, f)), per_expert((1, f)), per_expert((1, f)),
                  per_expert((f, d)), per_expert((1, d))],
        out_specs=pl.BlockSpec(memory_space=pl.ANY),
        scratch_shapes=[pltpu.VMEM((2, rows * chunks, LANES), F32), pltpu.VMEM((2, rows * chunks, LANES), F32),
                        pltpu.SemaphoreType.DMA((2,)), pltpu.SemaphoreType.DMA((2,)),
                        pltpu.SemaphoreType.DMA((1,))],
    )
    return pl.pallas_call(
        functools.partial(_expert_kernel, rows=rows, n_tok_rows=TOP_K * n, chunks=chunks),
        grid_spec=grid_spec,
        out_shape=jax.ShapeDtypeStruct(((TOP_K * n + 2 * rows) * chunks, LANES), F32),
        compiler_params=_cparams(("arbitrary",)),
        name="moe_experts",
    )(block_expert, n_used, src * chunks, src * chunks, dst_prev * chunks, x_tiles, w1g, w1l, b1g, b1l, w2, b2)


def _combine_kernel(x_ref, gate_ref, y0_ref, y1_ref, y2_ref, y3_ref, lg_ref, lb_ref, out_ref):
    gate = gate_ref[...]
    tm, d = x_ref.shape
    chunks = d // LANES
    cols = []
    for c in range(chunks):
        rows_c = pl.ds(c, tm, stride=chunks)
        cols.append(gate[:, 0:1] * y0_ref[rows_c, :] + gate[:, 1:2] * y1_ref[rows_c, :]
                    + gate[:, 2:3] * y2_ref[rows_c, :] + gate[:, 3:4] * y3_ref[rows_c, :])
    f = jnp.concatenate(cols, axis=1)
    out_ref[...] = _layer_norm(ALPHA * x_ref[...] + f, lg_ref[...], lb_ref[...])


def _moe_combine(xf, gate, y, ln_g, ln_b):
    n, d = xf.shape
    tm = min(ROW_TILE, n)
    tiles = n // tm
    row_tile = pl.BlockSpec((tm, d), lambda i: (i, 0))
    slot_tile = lambda j: pl.BlockSpec((tm * (d // LANES), LANES), lambda i: (j * tiles + i, 0))
    vec = pl.BlockSpec((1, d), lambda i: (0, 0))
    return pl.pallas_call(
        _combine_kernel,
        grid=(tiles,),
        in_specs=[row_tile, pl.BlockSpec((tm, TOP_K), lambda i: (i, 0))] + [slot_tile(j) for j in range(TOP_K)]
                 + [vec, vec],
        out_specs=row_tile,
        out_shape=jax.ShapeDtypeStruct((n, d), F32),
        compiler_params=_cparams(("parallel",)),
        name="moe_combine",
    )(xf, gate, y, y, y, y, ln_g.reshape(1, d), ln_b.reshape(1, d))


def _moe(xf, x_tiles, w_router, b_router, w1g, w1l, b1, w2, b2, ln_g, ln_b):
    n, d = xf.shape
    n_exp = w_router.shape[1]
    rows = min(EXPERT_ROWS, n)
    top_idx, gate = _router(xf, w_router, b_router)
    src, dst_prev, block_expert, n_used = _group_rows(top_idx, n_exp, rows)
    y_tiles = _moe_experts(x_tiles, src, dst_prev, block_expert, n_used, w1g, w1l, b1[:, None, 0::2],
                           b1[:, None, 1::2], w2.astype(BF16), b2[:, None, :])
    return _moe_combine(xf, gate, y_tiles, ln_g, ln_b)


def kernel(x, rw_mix, rw_w_rkv, rw_w0, rw_w1, rw_w2, rw_a0, rw_a1, rw_a2, rw_g1, rw_g2, rw_k_k, rw_k_a, rw_r_k,
           rw_lnx_g, rw_lnx_b, rw_w_o, sb_w_qkv, sb_w_o, moe_w_router, moe_b_router, moe_w1, moe_b1, moe_w2,
           moe_b2, ln_g, ln_b):
    batch, seq_len, d = x.shape
    xf = x.reshape(batch * seq_len, d)
    n_layers, n_exp = moe_w1.shape[:2]
    w1g, w1l = _swiglu_split(moe_w1.reshape((n_layers * n_exp,) + moe_w1.shape[2:]))
    w1g = w1g.reshape((n_layers, n_exp) + w1g.shape[1:])
    w1l = w1l.reshape((n_layers, n_exp) + w1l.shape[1:])
    for i in range(DEPTH):
        j = i // 2
        if i % 2 == 0:
            r, k, v, ld, a, g = _rwkv_proj(xf, seq_len, rw_mix[j], rw_w_rkv[j], rw_w0[j], rw_w1[j], rw_w2[j],
                                           rw_a0[j], rw_a1[j], rw_a2[j], rw_g1[j], rw_g2[j])
            o = _rwkv_scan(r, k, v, ld, a, seq_len, rw_k_k[j], rw_k_a[j], rw_r_k[j], rw_lnx_g[j], rw_lnx_b[j])
            xf, x_tiles = _proj_ln(xf, o, g, rw_w_o[j], ln_g[i, 0], ln_b[i, 0])
        else:
            q, k, v = _sb_qkv(xf, sb_w_qkv[j])
            o = _sb_attn(q, k, v, seq_len)
            xf, x_tiles = _proj_ln(xf, o, None, sb_w_o[j], ln_g[i, 0], ln_b[i, 0])
        xf = _moe(xf, x_tiles, moe_w_router[i], moe_b_router[i], w1g[i], w1l[i], moe_b1[i], moe_w2[i],
                  moe_b2[i], ln_g[i, 1], ln_b[i, 1])
    return xf.reshape(batch, seq_len, d)
```
